```python
import math
import jax
import jax.numpy as jnp
from jax import lax
import numpy as np

D_MODEL = 1024
BATCH = 8
SEQ = 2048
DEPTH = 2

GRID_W = 64
CTX_LEN = 256
HEAD_DIM = 64
GROUP_HEADS = 4
GROUP_WIDTH = GROUP_HEADS * HEAD_DIM
N_MIXERS = 4
MIX_WIDTH = N_MIXERS * GROUP_WIDTH
KV_HEADS = 2
KV_GROUPS = GROUP_HEADS // KV_HEADS
KV_WIDTH = KV_HEADS * HEAD_DIM
GDN_CONV = 3
GDN_CHUNK = 64
HGRN_CHUNK = 16
Q_BLOCK = 128
WINDOW = 128
ROPE_THETA = 10000.0
D_FF = 2816
FFN_CONV = 3
N_MOD = 6
EPS = 1e-6
NEG = -1e30
F32 = jnp.float32

IN_SIZES = (
    3 * GROUP_WIDTH,
    GROUP_WIDTH,
    2 * GROUP_HEADS,
    2 * GROUP_HEADS,
    GROUP_WIDTH,
    2 * KV_WIDTH,
    GROUP_WIDTH,
    2 * KV_WIDTH,
    GROUP_WIDTH,
    2 * GROUP_WIDTH,
    GROUP_WIDTH,
    GROUP_WIDTH,
)
IN_WIDTH = sum(IN_SIZES)

kernel_name = 'hybrid_parallel_heads_dit_block'


def rms_norm(x, gain):
    xf = x.astype(F32)
    y = xf * lax.rsqrt(jnp.mean(xf * xf, axis=-1, keepdims=True) + EPS)
    return (y * gain.astype(F32)).astype(x.dtype)


def l2_normalize(x):
    xf = x.astype(F32)
    return xf * lax.rsqrt(jnp.sum(xf * xf, axis=-1, keepdims=True) + EPS)


def split_heads(x, n_heads):
    b, length, _ = x.shape
    return x.reshape(b, length, n_heads, -1).transpose(0, 2, 1, 3)


def merge_heads(x):
    b, n, length, d = x.shape
    return x.transpose(0, 2, 1, 3).reshape(b, length, n * d)


def modulate(x, shift, scale):
    return x * (1.0 + scale) + shift


def split_columns(p):
    offsets = np.cumsum(IN_SIZES)[:-1].tolist()
    return jnp.split(p, offsets, axis=-1)


def dwconv_centered(x, w, b=None):
    k_width = w.shape[0]
    pad = k_width // 2
    length = x.shape[1]
    xp = jnp.pad(x, ((0, 0), (pad, pad), (0, 0)))
    y = xp[:, 0:length] * w[0]
    for j in range(1, k_width):
        y = y + xp[:, j:j + length] * w[j]
    return y if b is None else y + b


def axial_rope_tables(rows):
    row = jnp.repeat(jnp.arange(rows), GRID_W).astype(F32)
    col = jnp.tile(jnp.arange(GRID_W), rows).astype(F32)
    n_freq = HEAD_DIM // 4
    inv_freq = 1.0 / (ROPE_THETA ** (jnp.arange(n_freq, dtype=F32) / n_freq))
    ang = jnp.concatenate([row[:, None] * inv_freq, col[:, None] * inv_freq], axis=-1)
    return jnp.cos(ang), jnp.sin(ang)


def apply_rope(x, cos, sin):
    xf = x.astype(F32).reshape(x.shape[:-1] + (HEAD_DIM // 2, 2))
    x0, x1 = xf[..., 0], xf[..., 1]
    out = jnp.stack([x0 * cos - x1 * sin, x0 * sin + x1 * cos], axis=-1)
    return out.reshape(x.shape).astype(x.dtype)


def bidirectional_scan(scan_fn, ctx_fwd, ctx_bwd, lat_fwd, lat_bwd, s0):
    rev = lambda args: tuple(jnp.flip(a, axis=2) for a in args)
    o_cf, s_cf = scan_fn(*ctx_fwd, s0)
    o_lf, _ = scan_fn(*lat_fwd, s_cf)
    o_cb, s_cb = scan_fn(*rev(ctx_bwd), s0)
    o_lb, _ = scan_fn(*rev(lat_bwd), s_cb)
    return o_cf + jnp.flip(o_cb, axis=2), o_lf + jnp.flip(o_lb, axis=2)


def gdn_chunked(q, k, v, g, beta, s0):
    b, h, length, dk = q.shape
    dv = v.shape[-1]
    cs = GDN_CHUNK
    n = length // cs
    q = q.reshape(b, h, n, cs, dk)
    k = k.reshape(b, h, n, cs, dk)
    v = v.reshape(b, h, n, cs, dv)
    beta = beta.reshape(b, h, n, cs)
    gc = jnp.cumsum(g.reshape(b, h, n, cs), axis=-1)
    incl = jnp.tril(jnp.ones((cs, cs), bool))
    strict = jnp.tril(jnp.ones((cs, cs), bool), -1)
    decay = jnp.exp(jnp.where(incl, gc[..., :, None] - gc[..., None, :], NEG))
    kb = k * beta[..., None]
    m = jnp.where(strict, jnp.einsum('bhnid,bhnjd->bhnij', kb, k) * decay, 0.0)
    a = m + jnp.eye(cs, dtype=F32)
    rhs = jnp.concatenate([v * beta[..., None], kb * jnp.exp(gc)[..., None]], axis=-1)
    sol = lax.linalg.triangular_solve(a, rhs, left_side=True, lower=True, unit_diagonal=True)
    u, w = sol[..., :dv], sol[..., dv:]
    attn = jnp.einsum('bhnid,bhnjd->bhnij', q, k) * decay
    q_dec = q * jnp.exp(gc)[..., None]
    k_dec = k * jnp.exp(gc[..., -1:] - gc)[..., None]
    g_last = jnp.exp(gc[..., -1])

    def step(state, xs):
        u_i, w_i, qd_i, kd_i, at_i, gl_i = xs
        v_new = u_i - jnp.einsum('bhcd,bhde->bhce', w_i, state)
        o_i = jnp.einsum('bhcd,bhde->bhce', qd_i, state) + jnp.einsum('bhij,bhje->bhie', at_i, v_new)
        state = state * gl_i[..., None, None] + jnp.einsum('bhcd,bhce->bhde', kd_i, v_new)
        return state, o_i

    xs = tuple(jnp.moveaxis(t, 2, 0) for t in (u, w, q_dec, k_dec, attn, g_last))
    s_fin, o = lax.scan(step, s0, xs)
    return jnp.moveaxis(o, 0, 2).reshape(b, h, length, dv), s_fin


def hgrn_chunked(q, k, v, log_f, s0):
    b, h, length, dk = q.shape
    dv = v.shape[-1]
    cs = HGRN_CHUNK
    n = length // cs
    q = q.reshape(b, h, n, cs, dk)
    k = k.reshape(b, h, n, cs, dk)
    v = v.reshape(b, h, n, cs, dv)
    bc = jnp.cumsum(log_f.reshape(b, h, n, cs, dk), axis=3)
    incl = jnp.tril(jnp.ones((cs, cs), bool))[:, :, None]
    rel = bc[:, :, :, :, None, :] - bc[:, :, :, None, :, :]
    decay = jnp.exp(jnp.where(incl, rel, NEG))
    attn = jnp.einsum('bhnid,bhnjd,bhnijd->bhnij', q, k, decay)
    q_dec = q * jnp.exp(bc)
    k_dec = k * jnp.exp(bc[:, :, :, -1:] - bc)
    f_last = jnp.exp(bc[:, :, :, -1])

    def step(state, xs):
        qd_i, kd_i, v_i, at_i, fl_i = xs
        o_i = jnp.einsum('bhcd,bhde->bhce', qd_i, state) + jnp.einsum('bhij,bhje->bhie', at_i, v_i)
        state = state * fl_i[..., None] + jnp.einsum('bhcd,bhce->bhde', kd_i, v_i)
        return state, o_i

    xs = tuple(jnp.moveaxis(t, 2, 0) for t in (q_dec, k_dec, v, attn, f_last))
    s_fin, o = lax.scan(step, s0, xs)
    return jnp.moveaxis(o, 0, 2).reshape(b, h, length, dv), s_fin


def gdn_mixer(qkv_c, z_c, beta_c, a_c, qkv_l, z_l, beta_l, a_l, conv_w, a_log, dt_bias, o_gain):
    def prep(qkv, beta_raw, a_raw):
        b, length, _ = qkv.shape
        qkv = jax.nn.silu(dwconv_centered(qkv, conv_w))
        q, k, v = jnp.split(qkv, 3, axis=-1)
        q = l2_normalize(split_heads(q, GROUP_HEADS)) * HEAD_DIM ** -0.5
        k = l2_normalize(split_heads(k, GROUP_HEADS))
        v = split_heads(v, GROUP_HEADS).astype(F32)
        beta = jax.nn.sigmoid(beta_raw.astype(F32)).reshape(b, length, 2, GROUP_HEADS).transpose(2, 0, 3, 1)
        a = a_raw.astype(F32).reshape(b, length, 2, GROUP_HEADS) + dt_bias.astype(F32)
        g = (-jnp.exp(a_log.astype(F32)) * jax.nn.softplus(a)).transpose(2, 0, 3, 1)
        return q, k, v, g, beta

    qc, kc, vc, gc, bc = prep(qkv_c, beta_c, a_c)
    ql, kl, vl, gl, bl = prep(qkv_l, beta_l, a_l)
    s0 = jnp.zeros((qkv_l.shape[0], GROUP_HEADS, HEAD_DIM, HEAD_DIM), F32)
    o_c, o_l = bidirectional_scan(gdn_chunked,
                                  (qc, kc, vc, gc[0], bc[0]), (qc, kc, vc, gc[1], bc[1]),
                                  (ql, kl, vl, gl[0], bl[0]), (ql, kl, vl, gl[1], bl[1]), s0)

    def out(o, z):
        gate = jax.nn.silu(split_heads(z, GROUP_HEADS).astype(F32))
        return merge_heads(rms_norm(o, o_gain) * gate).astype(z.dtype)

    return out(o_c, z_c), out(o_l, z_l)


def softmax_attend(q, k, v):
    s = jnp.einsum('bkgqd,bksd->bkgqs', q, k).astype(F32) * HEAD_DIM ** -0.5
    p = jax.nn.softmax(s, axis=-1).astype(v.dtype)
    return jnp.einsum('bkgqs,bksd->bkgqd', p, v)


def global_gqa(q_c, kv_c, q_l, kv_l, q_gain, k_gain, cos, sin, with_ctx):
    b, length, _ = q_l.shape
    k_c, v_c = jnp.split(kv_c, 2, axis=-1)
    k_l, v_l = jnp.split(kv_l, 2, axis=-1)
    k_c = rms_norm(split_heads(k_c, KV_HEADS), k_gain)
    v_c = split_heads(v_c, KV_HEADS)
    k_l = apply_rope(rms_norm(split_heads(k_l, KV_HEADS), k_gain), cos, sin)
    v_l = split_heads(v_l, KV_HEADS)
    q_l = apply_rope(rms_norm(split_heads(q_l, GROUP_HEADS), q_gain), cos, sin)
    keys = jnp.concatenate([k_c, k_l], axis=2)
    vals = jnp.concatenate([v_c, v_l], axis=2)
    nb = length // Q_BLOCK
    qb = jnp.moveaxis(q_l.reshape(b, KV_HEADS, KV_GROUPS, nb, Q_BLOCK, HEAD_DIM), 3, 0)
    o_l = lax.map(lambda q_blk: softmax_attend(q_blk, keys, vals), qb)
    o_l = merge_heads(jnp.moveaxis(o_l, 0, 3).reshape(b, GROUP_HEADS, length, HEAD_DIM))
    if not with_ctx:
        return None, o_l
    qc = rms_norm(split_heads(q_c, GROUP_HEADS), q_gain).reshape(b, KV_HEADS, KV_GROUPS, -1, HEAD_DIM)
    o_c = softmax_attend(qc, k_c, v_c).reshape(b, GROUP_HEADS, -1, HEAD_DIM)
    return merge_heads(o_c), o_l


def window_gqa(q_c, kv_c, q_l, kv_l, sink, cos, sin, with_ctx):
    b, length, _ = q_l.shape
    nb = length // Q_BLOCK
    scale = HEAD_DIM ** -0.5
    k_c, v_c = jnp.split(kv_c, 2, axis=-1)
    k_l, v_l = jnp.split(kv_l, 2, axis=-1)
    k_c = split_heads(k_c, KV_HEADS)
    v_c = split_heads(v_c, KV_HEADS)
    k_l = apply_rope(split_heads(k_l, KV_HEADS), cos, sin)
    v_l = split_heads(v_l, KV_HEADS)
    q_l = apply_rope(split_heads(q_l, GROUP_HEADS), cos, sin)

    def band(t):
        tp = jnp.pad(t, ((0, 0), (0, 0), (Q_BLOCK, Q_BLOCK), (0, 0))).reshape(b, KV_HEADS, nb + 2, Q_BLOCK, HEAD_DIM)
        return jnp.concatenate([tp[:, :, :-2], tp[:, :, 1:-1], tp[:, :, 2:]], axis=3)

    k_band, v_band = band(k_l), band(v_l)
    qb = q_l.reshape(b, KV_HEADS, KV_GROUPS, nb, Q_BLOCK, HEAD_DIM)
    s_band = jnp.einsum('bkgnqd,bknsd->bkgnqs', qb, k_band).astype(F32) * scale
    blk = jnp.arange(nb)[:, None, None]
    q_pos = blk * Q_BLOCK + jnp.arange(Q_BLOCK)[None, :, None]
    k_pos = (blk - 1) * Q_BLOCK + jnp.arange(3 * Q_BLOCK)[None, None, :]
    valid = (jnp.abs(q_pos - k_pos) <= WINDOW) & (k_pos >= 0) & (k_pos < length)
    s_band = jnp.where(valid, s_band, NEG)
    s_ctx = jnp.einsum('bkgnqd,bksd->bkgnqs', qb, k_c).astype(F32) * scale
    sink_col = jnp.broadcast_to(sink.astype(F32).reshape(1, KV_HEADS, KV_GROUPS, 1, 1, 1), s_band.shape[:-1] + (1,))
    p = jax.nn.softmax(jnp.concatenate([s_band, s_ctx, sink_col], axis=-1), axis=-1).astype(v_l.dtype)
    n_band = 3 * Q_BLOCK
    n_ctx = k_c.shape[2]
    o_l = (jnp.einsum('bkgnqs,bknsd->bkgnqd', p[..., :n_band], v_band)
           + jnp.einsum('bkgnqs,bksd->bkgnqd', p[..., n_band:n_band + n_ctx], v_c))
    o_l = merge_heads(o_l.reshape(b, GROUP_HEADS, length, HEAD_DIM))
    if not with_ctx:
        return None, o_l
    qc = split_heads(q_c, GROUP_HEADS).reshape(b, KV_HEADS, KV_GROUPS, -1, HEAD_DIM)
    s_c = jnp.einsum('bkgqd,bksd->bkgqs', qc, k_c).astype(F32) * scale
    sink_c = jnp.broadcast_to(sink.astype(F32).reshape(1, KV_HEADS, KV_GROUPS, 1, 1), s_c.shape[:-1] + (1,))
    p_c = jax.nn.softmax(jnp.concatenate([s_c, sink_c], axis=-1), axis=-1)[..., :-1].astype(v_c.dtype)
    o_c = jnp.einsum('bkgqs,bksd->bkgqd', p_c, v_c).reshape(b, GROUP_HEADS, -1, HEAD_DIM)
    return merge_heads(o_c), o_l


def hgrn_mixer(q_c, f_c, i_c, g_c, q_l, f_l, i_l, g_l, lower_bound, o_gain):
    def prep(q, f, i):
        b, length, _ = q.shape
        q = split_heads(jax.nn.silu(q), GROUP_HEADS).astype(F32)
        v = split_heads(i, GROUP_HEADS).astype(F32)
        z = f.astype(F32).reshape(b, length, 2, GROUP_WIDTH)
        f_gate = lower_bound + (1.0 - lower_bound) * jax.nn.sigmoid(z)
        log_f = jnp.log(f_gate)
        log_f = log_f.reshape(b, length, 2, GROUP_HEADS, HEAD_DIM).transpose(2, 0, 3, 1, 4)
        k = -jnp.expm1(log_f)
        return q, v, k, log_f

    qc, vc, kc, lfc = prep(q_c, f_c, i_c)
    ql, vl, kl, lfl = prep(q_l, f_l, i_l)
    s0 = jnp.zeros((q_l.shape[0], GROUP_HEADS, HEAD_DIM, HEAD_DIM), F32)
    o_c, o_l = bidirectional_scan(hgrn_chunked,
                                  (qc, kc[0], vc, lfc[0]), (qc, kc[1], vc, lfc[1]),
                                  (ql, kl[0], vl, lfl[0]), (ql, kl[1], vl, lfl[1]), s0)

    def out(o, g):
        gate = jax.nn.silu(split_heads(g, GROUP_HEADS).astype(F32))
        return merge_heads(rms_norm(o, o_gain) * gate).astype(g.dtype)

    return out(o_c, g_c), out(o_l, g_l)


def token_mixing(h_c, h_l, w_in, w_out, gdn_conv_w, gdn_a_log, gdn_dt_bias, gdn_norm,
                 attn_q_norm, attn_k_norm, swa_sink, hgrn_lb, hgrn_norm, cos, sin, with_ctx):
    pc = split_columns(h_c @ w_in)
    pl = split_columns(h_l @ w_in)
    a_c, a_l = gdn_mixer(*pc[0:4], *pl[0:4], gdn_conv_w, gdn_a_log, gdn_dt_bias, gdn_norm)
    b_c, b_l = global_gqa(pc[4], pc[5], pl[4], pl[5], attn_q_norm, attn_k_norm, cos, sin, with_ctx)
    c_c, c_l = window_gqa(pc[6], pc[7], pl[6], pl[7], swa_sink, cos, sin, with_ctx)
    d_c, d_l = hgrn_mixer(*pc[8:12], *pl[8:12], hgrn_lb, hgrn_norm)
    o_l = jnp.concatenate([a_l, b_l, c_l, d_l], axis=-1) @ w_out
    if not with_ctx:
        return None, o_l
    o_c = jnp.concatenate([a_c, b_c, c_c, d_c], axis=-1) @ w_out
    return o_c, o_l


def conv_ffn(h, w_up, conv_w, conv_b, w_down):
    u = dwconv_centered(h @ w_up, conv_w, conv_b)
    gate, val = jnp.split(u, 2, axis=-1)
    return (jax.nn.silu(gate) * val) @ w_down


def setup_inputs(seed: int = 0) -> dict:
    key = jax.random.key(seed)
    ks = jax.random.split(key, 25)

    def normal(k, shape, scale):
        return jax.random.normal(k, shape, F32) * scale

    def gain(k, shape):
        return 1.0 + 0.05 * jax.random.normal(k, shape, F32)

    dt = jnp.exp(jax.random.uniform(ks[14], (DEPTH, 2, GROUP_HEADS), F32, math.log(1e-3), math.log(1e-1)))
    return {
        'x': normal(ks[0], (BATCH, SEQ, D_MODEL), 1.0),
        'c': normal(ks[1], (BATCH, D_MODEL), 1.0),
        'ctx': normal(ks[2], (BATCH, CTX_LEN, D_MODEL), 1.0),
        'c_ctx': normal(ks[3], (D_MODEL,), 1.0),
        'ada_w': normal(ks[4], (DEPTH, D_MODEL, N_MOD * D_MODEL), D_MODEL ** -0.5),
        'ada_b': normal(ks[5], (DEPTH, N_MOD * D_MODEL), 0.02),
        'norm_pre_mix': gain(ks[6], (DEPTH, D_MODEL)),
        'norm_post_mix': gain(ks[7], (DEPTH, D_MODEL)),
        'norm_pre_ffn': gain(ks[8], (DEPTH, D_MODEL)),
        'norm_post_ffn': gain(ks[9], (DEPTH, D_MODEL)),
        'w_in': normal(ks[10], (DEPTH, D_MODEL, IN_WIDTH), D_MODEL ** -0.5),
        'w_out': normal(ks[11], (DEPTH, MIX_WIDTH, D_MODEL), MIX_WIDTH ** -0.5),
        'gdn_conv_w': normal(ks[12], (DEPTH, GDN_CONV, 3 * GROUP_WIDTH), GDN_CONV ** -0.5),
        'gdn_a_log': jnp.log(jax.random.uniform(ks[13], (DEPTH, 2, GROUP_HEADS), F32, 1.0, 16.0)),
        'gdn_dt_bias': dt + jnp.log(-jnp.expm1(-dt)),
        'gdn_norm': gain(ks[15], (DEPTH, HEAD_DIM)),
        'attn_q_norm': gain(ks[16], (DEPTH, HEAD_DIM)),
        'attn_k_norm': gain(ks[17], (DEPTH, HEAD_DIM)),
        'swa_sink': normal(ks[18], (DEPTH, GROUP_HEADS), 0.5),
        'hgrn_lb_raw': normal(ks[19], (DEPTH, 2, GROUP_WIDTH), 0.5),
        'hgrn_norm': gain(ks[20], (DEPTH, HEAD_DIM)),
        'ffn_w_up': normal(ks[21], (DEPTH, D_MODEL, 2 * D_FF), D_MODEL ** -0.5),
        'ffn_conv_w': normal(ks[22], (DEPTH, FFN_CONV, 2 * D_FF), FFN_CONV ** -0.5),
        'ffn_conv_b': normal(ks[23], (DEPTH, 2 * D_FF), 0.02),
        'ffn_w_down': normal(ks[24], (DEPTH, D_FF, D_MODEL), D_FF ** -0.5),
    }


def reference(x, c, ctx, c_ctx, ada_w, ada_b, norm_pre_mix, norm_post_mix, norm_pre_ffn, norm_post_ffn,
              w_in, w_out, gdn_conv_w, gdn_a_log, gdn_dt_bias, gdn_norm, attn_q_norm, attn_k_norm,
              swa_sink, hgrn_lb_raw, hgrn_norm, ffn_w_up, ffn_conv_w, ffn_conv_b, ffn_w_down):
    rows = x.shape[1] // GRID_W
    cos, sin = axial_rope_tables(rows)
    lb_p = jax.nn.softmax(hgrn_lb_raw.astype(F32), axis=0)
    hgrn_lb = jnp.cumsum(lb_p, axis=0) - lb_p[0]
    c_act = jax.nn.silu(c)
    c_ctx_act = jax.nn.silu(c_ctx)
    x_l, x_c = x, ctx
    for layer in range(DEPTH):
        with_ctx = layer < DEPTH - 1
        mod_l = jnp.split((c_act @ ada_w[layer] + ada_b[layer])[:, None, :], N_MOD, axis=-1)
        mod_c = jnp.split(c_ctx_act @ ada_w[layer] + ada_b[layer], N_MOD, axis=-1)
        h_l = modulate(rms_norm(x_l, norm_pre_mix[layer]), mod_l[0], mod_l[1])
        h_c = modulate(rms_norm(x_c, norm_pre_mix[layer]), mod_c[0], mod_c[1])
        o_c, o_l = token_mixing(h_c, h_l, w_in[layer], w_out[layer], gdn_conv_w[layer], gdn_a_log[layer],
                                gdn_dt_bias[layer], gdn_norm[layer], attn_q_norm[layer], attn_k_norm[layer],
                                swa_sink[layer], hgrn_lb[layer], hgrn_norm[layer], cos, sin, with_ctx)
        x_l = x_l + mod_l[2] * rms_norm(o_l, norm_post_mix[layer])
        f_l = conv_ffn(modulate(rms_norm(x_l, norm_pre_ffn[layer]), mod_l[3], mod_l[4]),
                       ffn_w_up[layer], ffn_conv_w[layer], ffn_conv_b[layer], ffn_w_down[layer])
        x_l = x_l + mod_l[5] * rms_norm(f_l, norm_post_ffn[layer])
        if with_ctx:
            x_c = x_c + mod_c[2] * rms_norm(o_c, norm_post_mix[layer])
            f_c = conv_ffn(modulate(rms_norm(x_c, norm_pre_ffn[layer]), mod_c[3], mod_c[4]),
                           ffn_w_up[layer], ffn_conv_w[layer], ffn_conv_b[layer], ffn_w_down[layer])
            x_c = x_c + mod_c[5] * rms_norm(f_c, norm_post_ffn[layer])
    return x_l
```

```python
import functools
import math

import numpy as np
import jax
import jax.numpy as jnp
from jax import lax
from jax.experimental import pallas as pl
from jax.experimental.pallas import tpu as pltpu

F32 = jnp.float32
BF16 = jnp.bfloat16

HEAD_DIM = 64
GROUP_HEADS = 4
GROUP_WIDTH = GROUP_HEADS * HEAD_DIM
KV_HEADS = 2
KV_WIDTH = KV_HEADS * HEAD_DIM
GRID_W = 64
GDN_CHUNK = 64
HGRN_CHUNK = 16
Q_BLOCK = 128
WINDOW = 128
ROPE_THETA = 10000.0
N_MOD = 6
EPS = 1e-6
NEG = -1e30
TM = 256
VMEM_LIMIT = 56 * 1024 * 1024

COL_GDN_QKV = 0
COL_GDN_Z = 768
COL_ATT_Q = 1024
COL_ATT_KV = 1280
COL_SWA_Q = 1536
COL_SWA_KV = 1792
COL_HG_Q = 2048
COL_HG_F = 2304
COL_HG_I = 2816
COL_HG_G = 3072
COL_GDN_BA = 3328
P_WIDTH = 3456


def _cparams(n_axes):
    return pltpu.CompilerParams(dimension_semantics=("arbitrary",) * n_axes,
                                vmem_limit_bytes=VMEM_LIMIT)


def _dot(a, b):
    return jnp.dot(a, b, preferred_element_type=F32)


def _mm(a, b):
    return _dot(a.astype(BF16), b.astype(BF16))


def _mm_nt(a, b):
    return lax.dot_general(a.astype(BF16), b.astype(BF16), (((1,), (1,)), ((), ())),
                           preferred_element_type=F32)


def _mm_tn(a, b):
    return lax.dot_general(a.astype(BF16), b.astype(BF16), (((0,), (0,)), ((), ())),
                           preferred_element_type=F32)


def _split(a, parts):
    out = []
    r = a
    for i in range(parts):
        p = r.astype(BF16)
        out.append(p)
        if i + 1 < parts:
            r = r - p.astype(F32)
    return out


def _mm_x01(a, b01, parts=3):
    m = a.shape[0]
    r = _dot(jnp.concatenate(_split(a, parts), axis=0), b01)
    return sum(r[i * m:(i + 1) * m] for i in range(parts))


def _mm_01x(a01, b, parts=3):
    n = b.shape[1]
    r = _dot(a01, jnp.concatenate(_split(b, parts), axis=1))
    return sum(r[:, i * n:(i + 1) * n] for i in range(parts))


def _mm_hi(a, b):
    ah, al = _split(a, 2)
    bh, bl = _split(b, 2)
    return _dot(ah, bh) + _dot(al, bh) + _dot(ah, bl)


def _silu(x):
    return x * jax.nn.sigmoid(x)


def _softplus(x):
    return jnp.maximum(x, 0.0) + jnp.log1p(jnp.exp(-jnp.abs(x)))


def _head_sumsq(x, ones_bd):
    return _mm_x01(x * x, ones_bd, parts=2)


def _rope(x, cos, sin_signed):
    w = x.shape[-1]
    lane = lax.broadcasted_iota(jnp.int32, x.shape, x.ndim - 1)
    nxt = pltpu.roll(x, w - 1, x.ndim - 1)
    prv = pltpu.roll(x, 1, x.ndim - 1)
    swapped = jnp.where((lane & 1) == 0, nxt, prv)
    return x * cos + swapped * sin_signed


def _block_diag_ones(n, blk):
    i = np.arange(n)
    return (i[:, None] // blk == i[None, :] // blk).astype(np.float32)


def _gdn_consts():
    h64 = _block_diag_ones(GROUP_WIDTH, HEAD_DIM)
    i = np.arange(GDN_CHUNK)
    tri = np.stack([(i[None, :] <= i[:, None]), (i[None, :] >= i[:, None])]).astype(np.float32)
    tri_ones = np.concatenate([tri, np.ones((2, GDN_CHUNK, GDN_CHUNK), np.float32)], axis=1)
    ea = np.zeros((2, 128, GROUP_WIDTH), np.float32)
    eb = np.zeros((2, 128, GROUP_WIDTH), np.float32)
    pa = np.zeros((2, 128, GROUP_HEADS * GROUP_WIDTH), np.float32)
    pb = np.zeros((2, 128, GROUP_HEADS * GROUP_WIDTH), np.float32)
    for d in range(2):
        for h in range(GROUP_HEADS):
            eb[d, d * 4 + h, h * 64:(h + 1) * 64] = 1.0
            ea[d, 8 + d * 4 + h, h * 64:(h + 1) * 64] = 1.0
            pb[d, d * 4 + h, h * 256:(h + 1) * 256] = 1.0
            pa[d, 8 + d * 4 + h, h * 256:(h + 1) * 256] = 1.0
    r = np.arange(GROUP_WIDTH)
    same = r[:, None] // 64 == r[None, :] // 64
    ri, ci = r[:, None] % 64, r[None, :] % 64
    incl = np.stack([same & (ci <= ri), same & (ci >= ri)]).astype(np.float32)
    strict = np.stack([same & (ci < ri), same & (ci > ri)]).astype(np.float32)
    return dict(
        hm=jnp.asarray(h64, F32), hm_b=jnp.asarray(h64, BF16),
        tri_ones=jnp.asarray(tri_ones, BF16),
        ea=jnp.asarray(ea, BF16), eb=jnp.asarray(eb, BF16),
        pa=jnp.asarray(pa, BF16), pb=jnp.asarray(pb, BF16),
        ones=jnp.ones((GROUP_WIDTH, GROUP_WIDTH), BF16),
        eye=jnp.asarray(np.eye(GROUP_WIDTH), F32),
        incl=jnp.asarray(incl, F32), strict=jnp.asarray(strict, F32),
        lvl=jnp.asarray(np.stack([((r[:, None] >> (l + 1)) == (r[None, :] >> (l + 1)))
                                  & ((r[:, None] >> l) != (r[None, :] >> l)) for l in range(6)]), F32),
    )


def _hgrn_consts():
    h64 = _block_diag_ones(TM, HEAD_DIM)
    r = np.arange(TM)
    same = r[:, None] // HGRN_CHUNK == r[None, :] // HGRN_CHUNK
    tri = np.stack([same & (r[None, :] <= r[:, None]), same & (r[None, :] >= r[:, None])]).astype(np.float32)
    tri_ones = np.concatenate([tri, np.broadcast_to(same.astype(np.float32), (2, TM, TM))], axis=1)
    return dict(hm=jnp.asarray(h64, F32), hm_b=jnp.asarray(h64, BF16),
                tri_ones=jnp.asarray(tri_ones, BF16))


def _full_spec(arr):
    nd = arr.ndim
    return pl.BlockSpec(arr.shape, lambda *_: (0,) * nd)


def _ada_kernel(c_ref, w_ref, b_ref, o_ref):
    act = _silu(c_ref[...])
    o_ref[0] = _mm_hi(act, w_ref[0]) + b_ref[0]


def _ada_mod(cstack, ada_w, ada_b):
    depth, d, n = ada_w.shape
    tn = 1536
    rows = cstack.shape[0]
    return pl.pallas_call(
        _ada_kernel,
        grid=(depth, n // tn),
        in_specs=[pl.BlockSpec((rows, d), lambda l, j: (0, 0)),
                  pl.BlockSpec((1, d, tn), lambda l, j: (l, 0, j)),
                  pl.BlockSpec((1, 1, tn), lambda l, j: (l, 0, j))],
        out_specs=pl.BlockSpec((1, rows, tn), lambda l, j: (l, 0, j)),
        out_shape=jax.ShapeDtypeStruct((depth, rows, n), F32),
        compiler_params=_cparams(2),
        name="ada_mod",
    )(cstack, ada_w, ada_b.reshape(depth, 1, n))


def _rms(x, gain):
    return x * lax.rsqrt(jnp.mean(x * x, axis=-1, keepdims=True) + EPS) * gain


def _inproj_kernel(x_ref, mod_ref, gain_ref, w_ref, o_ref):
    mod = mod_ref[0, 0]
    h = _rms(x_ref[0], gain_ref[...]) * (1.0 + mod[1:2]) + mod[0:1]
    o_ref[0] = _dot(h.astype(BF16), w_ref[...])


def _inproj(xs, modt, gain, w_bf16):
    b, s, d = xs.shape
    n = w_bf16.shape[1]
    return pl.pallas_call(
        _inproj_kernel,
        grid=(b, s // TM),
        in_specs=[pl.BlockSpec((1, TM, d), lambda i, t: (i, t, 0)),
                  pl.BlockSpec((1, 1, N_MOD, d), lambda i, t: (i, jnp.minimum(t, 1), 0, 0)),
                  pl.BlockSpec((1, d), lambda i, t: (0, 0)),
                  pl.BlockSpec((d, n), lambda i, t: (0, 0))],
        out_specs=pl.BlockSpec((1, TM, n), lambda i, t: (i, t, 0)),
        out_shape=jax.ShapeDtypeStruct((b, s, n), F32),
        compiler_params=_cparams(2),
        name="inproj",
    )(xs, modt, gain.reshape(1, d), w_bf16)


def _gdn_kernel(qkv_ref, z_ref, ba_ref, cw_ref, alog_ref, dt_ref, gain_ref,
                hm_ref, hmb_ref, trio_ref, ea_ref, eb_ref, pa_ref, pb_ref, ones_ref, eye_ref,
                incl_ref, strict_ref, lvl_ref, o_ref,
                q_s, k_s, v_s, of_s, ob_s, st_s, *, n_ctx):
    s = qkv_ref.shape[1]
    gw = GROUP_WIDTH
    n_tiles = s // TM
    n_chunks = s // GDN_CHUNK
    nc_ctx = n_ctx // GDN_CHUNK

    def conv_tile(t, carry):
        r0 = pl.multiple_of(t * TM, TM)
        x = qkv_ref[0, pl.ds(r0, TM), :]
        prev8 = qkv_ref[0, pl.ds(pl.multiple_of(jnp.maximum(r0 - 8, 0), 8), 8), :]
        next8 = qkv_ref[0, pl.ds(pl.multiple_of(jnp.minimum(r0 + TM, s - 8), 8), 8), :]
        zero_prev = (r0 == 0) | (r0 == n_ctx)
        zero_next = (r0 + TM == n_ctx) | (r0 + TM == s)
        prev_row = jnp.where(zero_prev, 0.0, prev8[7:8])
        next_row = jnp.where(zero_next, 0.0, next8[0:1])
        row = lax.broadcasted_iota(jnp.int32, x.shape, 0)
        x_dn = jnp.where(row == 0, prev_row, pltpu.roll(x, 1, 0))
        x_up = jnp.where(row == TM - 1, next_row, pltpu.roll(x, TM - 1, 0))
        y = _silu(x_dn * cw_ref[0:1] + x * cw_ref[1:2] + x_up * cw_ref[2:3])
        q, k, v = y[:, 0:gw], y[:, gw:2 * gw], y[:, 2 * gw:3 * gw]
        hmb = hmb_ref[...]
        q_s[pl.ds(r0, TM), :] = q * lax.rsqrt(_head_sumsq(q, hmb) + EPS) * (HEAD_DIM ** -0.5)
        k_s[pl.ds(r0, TM), :] = k * lax.rsqrt(_head_sumsq(k, hmb) + EPS)
        v_s[pl.ds(r0, TM), :] = v
        return carry

    lax.fori_loop(0, n_tiles, conv_tile, 0)
    st_s[...] = jnp.zeros_like(st_s)
    neg_a = -jnp.exp(alog_ref[...])
    dt_row = dt_ref[...]

    def stack4(x):
        return jnp.concatenate([x] * GROUP_HEADS, axis=0) * hm_ref[...]

    def unstack4(x):
        return sum(x[h * GDN_CHUNK:(h + 1) * GDN_CHUNK] for h in range(GROUP_HEADS))

    def restack(x):
        return jnp.concatenate([x[:, h * gw:(h + 1) * gw] for h in range(GROUP_HEADS)], axis=0)

    def chunk_dir(ci, d, out_s):
        r0 = pl.multiple_of(ci * GDN_CHUNK, GDN_CHUNK)
        q = q_s[pl.ds(r0, GDN_CHUNK), :]
        k = k_s[pl.ds(r0, GDN_CHUNK), :]
        v = v_s[pl.ds(r0, GDN_CHUNK), :]
        ba = ba_ref[0, pl.ds(r0, GDN_CHUNK), :]
        beta_c = jax.nn.sigmoid(ba)
        g_c = neg_a * _softplus(ba + dt_row)
        cums = _mm_01x(trio_ref[d], g_c)
        gc_c, gtot_c = cums[0:GDN_CHUNK], cums[GDN_CHUNK:]
        nat = _mm_x01(jnp.concatenate([gc_c, gtot_c], axis=0), ea_ref[d])
        gn_gc, gn_tot = nat[0:GDN_CHUNK], nat[GDN_CHUNK:]
        gn_beta = _mm_x01(beta_c, eb_ref[d])
        c_gc = restack(_mm_x01(gc_c, pa_ref[d]))
        c_beta = restack(_mm_x01(beta_c, pb_ref[d]))
        cdiag = jnp.concatenate(_split(c_gc * eye_ref[...], 3), axis=0)
        r3 = lax.dot_general(ones_ref[...], cdiag, (((1,), (1,)), ((), ())), preferred_element_type=F32)
        r_gc = r3[:, 0:gw] + r3[:, gw:2 * gw] + r3[:, 2 * gw:3 * gw]
        decay = jnp.exp(jnp.where(incl_ref[d] > 0.5, c_gc - r_gc, NEG))
        kst = stack4(k).astype(BF16)
        kk = lax.dot_general(kst, kst, (((1,), (1,)), ((), ())), preferred_element_type=F32)
        qk = lax.dot_general(stack4(q).astype(BF16), kst, (((1,), (1,)), ((), ())), preferred_element_type=F32)
        mmat = strict_ref[d] * (c_beta * kk * decay)
        tmat = eye_ref[...] - mmat * lvl_ref[0]
        for lv in range(1, 6):
            tc = _mm_hi(tmat, mmat * lvl_ref[lv])
            tmat = tmat - _mm_hi(tc, tmat)
        e_gc = jnp.exp(gn_gc)
        vb = v * gn_beta
        kbg = k * gn_beta * e_gc
        rhs = jnp.concatenate([stack4(vb), stack4(kbg)], axis=1)
        sol = _mm_hi(tmat, rhs)
        u = unstack4(sol[:, 0:gw])
        w = unstack4(sol[:, gw:2 * gw])
        q_dec = q * e_gc
        k_dec = k * jnp.exp(gn_tot - gn_gc)
        state = st_s[d]
        v_new = u - _mm(w, state)
        attn = qk * decay
        o = _mm(q_dec, state) + unstack4(_mm(attn, stack4(v_new)))
        out_s[pl.ds(r0, GDN_CHUNK), :] = o
        g_last = jnp.exp(gn_tot[0:1])
        st_s[d] = state * g_last + hm_ref[...] * _mm_tn(k_dec, v_new)

    def step(p, carry):
        chunk_dir(p, 0, of_s)
        cb = jnp.where(p < nc_ctx, nc_ctx - 1 - p, n_chunks - 1 + nc_ctx - p)
        chunk_dir(cb, 1, ob_s)
        return carry

    lax.fori_loop(0, n_chunks, step, 0)

    def out_tile(t, carry):
        r0 = pl.multiple_of(t * TM, TM)
        o = of_s[pl.ds(r0, TM), :] + ob_s[pl.ds(r0, TM), :]
        ms = _head_sumsq(o, hmb_ref[...]) * (1.0 / HEAD_DIM)
        y = o * lax.rsqrt(ms + EPS) * gain_ref[...]
        o_ref[0, pl.ds(r0, TM), :] = (y * _silu(z_ref[0, pl.ds(r0, TM), :])).astype(o_ref.dtype)
        return carry

    lax.fori_loop(0, n_tiles, out_tile, 0)


def _gdn(p, conv_w, a_log, dt_bias, o_gain, n_ctx):
    b, s, _ = p.shape
    gw = GROUP_WIDTH
    cst = _gdn_consts()
    alog_row = jnp.zeros((1, 128), F32).at[0, 8:16].set(a_log.reshape(-1))
    dt_row = jnp.zeros((1, 128), F32).at[0, 8:16].set(dt_bias.reshape(-1))
    gain_row = jnp.tile(o_gain, GROUP_HEADS).reshape(1, gw)
    names = ["hm", "hm_b", "tri_ones", "ea", "eb", "pa", "pb", "ones", "eye", "incl", "strict", "lvl"]
    consts = [cst[n] for n in names]
    return pl.pallas_call(
        functools.partial(_gdn_kernel, n_ctx=n_ctx),
        grid=(b,),
        in_specs=[pl.BlockSpec((1, s, 3 * gw), lambda i: (i, 0, COL_GDN_QKV // (3 * gw))),
                  pl.BlockSpec((1, s, gw), lambda i: (i, 0, COL_GDN_Z // gw)),
                  pl.BlockSpec((1, s, 128), lambda i: (i, 0, COL_GDN_BA // 128)),
                  _full_spec(conv_w), _full_spec(alog_row), _full_spec(dt_row), _full_spec(gain_row)]
                 + [_full_spec(c) for c in consts],
        out_specs=pl.BlockSpec((1, s, gw), lambda i: (i, 0, 0)),
        out_shape=jax.ShapeDtypeStruct((b, s, gw), BF16),
        scratch_shapes=[pltpu.VMEM((s, gw), F32)] * 5 + [pltpu.VMEM((2, gw, gw), F32)],
        compiler_params=_cparams(1),
        name="gdn_mixer",
    )(p, p, p, conv_w, alog_row, dt_row, gain_row, *consts)


def _hgrn_kernel(q_ref, f0_ref, f1_ref, i_ref, g_ref, lbraw_ref, gain_ref, hm_ref, hmb_ref, trio_ref,
                 o_ref, qd_s, kd_s, fl_s, oacc_s, st_s, *, n_ctx, layer):
    s = q_ref.shape[1]
    n_tiles = s // TM
    n_chunks = s // HGRN_CHUNK
    nc_ctx = n_ctx // HGRN_CHUNK
    depth = lbraw_ref.shape[0]

    raw = [lbraw_ref[l] for l in range(depth)]
    mx = functools.reduce(jnp.maximum, raw)
    ex = [jnp.exp(r - mx) for r in raw]
    lb_all = sum(ex[1:layer + 1], jnp.zeros_like(mx)) / sum(ex)

    f_refs = (f0_ref, f1_ref)

    def prep_tile(t, carry):
        r0 = pl.multiple_of(t * TM, TM)
        q = _silu(q_ref[0, pl.ds(r0, TM), :])
        v = i_ref[0, pl.ds(r0, TM), :]
        pos = lax.broadcasted_iota(jnp.int32, (TM, GROUP_WIDTH), 0) % HGRN_CHUNK
        o_acc = jnp.zeros((TM, GROUP_WIDTH), F32)
        for d in range(2):
            lb = lb_all[d:d + 1]
            z = f_refs[d][0, pl.ds(r0, TM), :]
            f = lb + (1.0 - lb) * jax.nn.sigmoid(z)
            logf = jnp.log(f)
            k = (1.0 - lb) * jax.nn.sigmoid(-z)
            cums = _mm_01x(trio_ref[d], logf)
            bc, tot = cums[0:TM], cums[TM:]
            qd_s[d, pl.ds(r0, TM), :] = (q * jnp.exp(bc)).astype(BF16)
            kd_s[d, pl.ds(r0, TM), :] = (k * jnp.exp(tot - bc)).astype(BF16)
            fl_s[d, pl.ds(r0, TM), :] = jnp.exp(tot)
            for delta in range(HGRN_CHUNK):
                if d == 0:
                    sh = delta
                    ok = pos >= delta
                else:
                    sh = (TM - delta) % TM
                    ok = pos + delta <= HGRN_CHUNK - 1
                if delta == 0:
                    k_sh, bc_sh, v_sh = k, bc, v
                else:
                    k_sh = pltpu.roll(k, sh, 0)
                    bc_sh = pltpu.roll(bc, sh, 0)
                    v_sh = pltpu.roll(v, sh, 0)
                e = q * k_sh * jnp.exp(jnp.where(ok, bc - bc_sh, NEG))
                o_acc = o_acc + _mm_x01(e, hmb_ref[...], parts=2) * v_sh
        oacc_s[pl.ds(r0, TM), :] = o_acc
        return carry

    lax.fori_loop(0, n_tiles, prep_tile, 0)
    st_s[...] = jnp.zeros_like(st_s)

    def chunk_dir(ci, d):
        r0 = pl.multiple_of(ci * HGRN_CHUNK, HGRN_CHUNK)
        qd = qd_s[d, pl.ds(r0, HGRN_CHUNK), :]
        kd = kd_s[d, pl.ds(r0, HGRN_CHUNK), :]
        v = i_ref[0, pl.ds(r0, HGRN_CHUNK), :].astype(BF16)
        fl = fl_s[d, pl.ds(r0, 1), :]
        st = st_s[d]
        o = lax.dot_general(qd, st.astype(BF16), (((1,), (1,)), ((), ())), preferred_element_type=F32)
        oacc_s[pl.ds(r0, HGRN_CHUNK), :] = oacc_s[pl.ds(r0, HGRN_CHUNK), :] + o
        upd = lax.dot_general(v, kd, (((0,), (0,)), ((), ())), preferred_element_type=F32)
        st_s[d] = st * fl + hm_ref[...] * upd

    def step(p, carry):
        chunk_dir(p, 0)
        cb = jnp.where(p < nc_ctx, nc_ctx - 1 - p, n_chunks - 1 + nc_ctx - p)
        chunk_dir(cb, 1)
        return carry

    lax.fori_loop(0, n_chunks, step, 0)

    def out_tile(t, carry):
        r0 = pl.multiple_of(t * TM, TM)
        o = oacc_s[pl.ds(r0, TM), :]
        ms = _head_sumsq(o, hmb_ref[...]) * (1.0 / HEAD_DIM)
        y = o * lax.rsqrt(ms + EPS) * gain_ref[...]
        o_ref[0, pl.ds(r0, TM), :] = (y * _silu(g_ref[0, pl.ds(r0, TM), :])).astype(o_ref.dtype)
        return carry

    lax.fori_loop(0, n_tiles, out_tile, 0)


def _hgrn(p, lb_raw, o_gain, n_ctx, layer):
    b, s, _ = p.shape
    gw = GROUP_WIDTH
    cst = _hgrn_consts()
    gain_row = jnp.tile(o_gain, GROUP_HEADS).reshape(1, gw)
    consts = [cst["hm"], cst["hm_b"], cst["tri_ones"]]

    def col(c):
        return pl.BlockSpec((1, s, gw), lambda i: (i, 0, c // gw))

    return pl.pallas_call(
        functools.partial(_hgrn_kernel, n_ctx=n_ctx, layer=layer),
        grid=(b,),
        in_specs=[col(COL_HG_Q), col(COL_HG_F), col(COL_HG_F + gw), col(COL_HG_I), col(COL_HG_G),
                  _full_spec(lb_raw), _full_spec(gain_row)] + [_full_spec(c) for c in consts],
        out_specs=pl.BlockSpec((1, s, gw), lambda i: (i, 0, 0)),
        out_shape=jax.ShapeDtypeStruct((b, s, gw), BF16),
        scratch_shapes=[pltpu.VMEM((2, s, gw), BF16), pltpu.VMEM((2, s, gw), BF16),
                        pltpu.VMEM((2, s, gw), F32), pltpu.VMEM((s, gw), F32),
                        pltpu.VMEM((2, gw, gw), F32)],
        compiler_params=_cparams(1),
        name="hgrn_mixer",
    )(p, p, p, p, p, lb_raw, gain_row, *consts)


def _gattn_kernel(q_ref, kv_ref, cq_ref, sq_ref, ck_ref, sk_ref, qg_ref, kg_ref, hm4_ref, hm2_ref,
                  o_ref, k_s, v_s, *, n_ctx, t0):
    s = kv_ref.shape[1]
    t = pl.program_id(1)

    @pl.when(t == 0)
    def _():
        def prep(j, carry):
            r0 = pl.multiple_of(j * TM, TM)
            kv = kv_ref[0, pl.ds(r0, TM), :]
            k = kv[:, 0:KV_WIDTH]
            ms = _head_sumsq(k, hm2_ref[...]) * (1.0 / HEAD_DIM)
            kn = k * lax.rsqrt(ms + EPS) * kg_ref[...]
            kn = _rope(kn, ck_ref[pl.ds(r0, TM), :], sk_ref[pl.ds(r0, TM), :])
            k_s[pl.ds(r0, TM), :] = kn.astype(BF16)
            v_s[pl.ds(r0, TM), :] = kv[:, KV_WIDTH:].astype(BF16)
            return carry
        lax.fori_loop(0, s // TM, prep, 0)

    q = q_ref[0]
    ms = _head_sumsq(q, hm4_ref[...]) * (1.0 / HEAD_DIM)
    qn = q * lax.rsqrt(ms + EPS) * qg_ref[...]
    qn = (_rope(qn, cq_ref[...], sq_ref[...]) * (HEAD_DIM ** -0.5)).astype(BF16)

    def attend(nk):
        outs = []
        for h in range(GROUP_HEADS):
            kvh = h // (GROUP_HEADS // KV_HEADS)
            kh = k_s[0:nk, kvh * HEAD_DIM:(kvh + 1) * HEAD_DIM]
            vh = v_s[0:nk, kvh * HEAD_DIM:(kvh + 1) * HEAD_DIM]
            sc = lax.dot_general(qn[:, h * HEAD_DIM:(h + 1) * HEAD_DIM], kh, (((1,), (1,)), ((), ())),
                                 preferred_element_type=F32)
            m = jnp.max(sc, axis=-1, keepdims=True)
            pr = jnp.exp(sc - m)
            l = jnp.sum(pr, axis=-1, keepdims=True)
            outs.append(_dot(pr.astype(BF16), vh) / l)
        o_ref[0] = jnp.concatenate(outs, axis=-1).astype(o_ref.dtype)

    if t0 == 0:
        @pl.when(t < n_ctx // TM)
        def _():
            attend(n_ctx)

        @pl.when(t >= n_ctx // TM)
        def _():
            attend(s)
    else:
        attend(s)


def _gattn(p, cos256, sin256, q_gain, k_gain, n_ctx, with_ctx):
    b, s, _ = p.shape
    gw = GROUP_WIDTH
    t0 = 0 if with_ctx else n_ctx // TM
    nt = s // TM - t0
    qg = jnp.tile(q_gain, GROUP_HEADS).reshape(1, gw)
    kg = jnp.tile(k_gain, KV_HEADS).reshape(1, KV_WIDTH)
    hm4 = jnp.asarray(_block_diag_ones(gw, HEAD_DIM), BF16)
    hm2 = jnp.asarray(_block_diag_ones(KV_WIDTH, HEAD_DIM), BF16)
    ck, sk = cos256[:, :KV_WIDTH], sin256[:, :KV_WIDTH]
    return pl.pallas_call(
        functools.partial(_gattn_kernel, n_ctx=n_ctx, t0=t0),
        grid=(b, nt),
        in_specs=[pl.BlockSpec((1, TM, gw), lambda i, t: (i, t + t0, COL_ATT_Q // gw)),
                  pl.BlockSpec((1, s, gw), lambda i, t: (i, 0, COL_ATT_KV // gw)),
                  pl.BlockSpec((TM, gw), lambda i, t: (t + t0, 0)),
                  pl.BlockSpec((TM, gw), lambda i, t: (t + t0, 0)),
                  _full_spec(ck), _full_spec(sk), _full_spec(qg), _full_spec(kg),
                  _full_spec(hm4), _full_spec(hm2)],
        out_specs=pl.BlockSpec((1, TM, gw), lambda i, t: (i, t, 0)),
        out_shape=jax.ShapeDtypeStruct((b, nt * TM, gw), BF16),
        scratch_shapes=[pltpu.VMEM((s, KV_WIDTH), BF16), pltpu.VMEM((s, KV_WIDTH), BF16)],
        compiler_params=_cparams(2),
        name="global_gqa",
    )(p, p, cos256, sin256, ck, sk, qg, kg, hm4, hm2)


def _wattn_kernel(q_ref, kv_ref, cq_ref, sq_ref, ck_ref, sk_ref, sink_ref, o_ref, k_s, v_s, *, n_ctx, t0):
    s = kv_ref.shape[1]
    qb = Q_BLOCK
    band_w = 3 * qb
    t = pl.program_id(1)
    scale = HEAD_DIM ** -0.5

    @pl.when(t == 0)
    def _():
        def prep(j, carry):
            r0 = pl.multiple_of(j * TM, TM)
            kv = kv_ref[0, pl.ds(r0, TM), :]
            kr = _rope(kv[:, 0:KV_WIDTH], ck_ref[pl.ds(r0, TM), :], sk_ref[pl.ds(r0, TM), :])
            k_s[pl.ds(r0, TM), :] = kr.astype(BF16)
            v_s[pl.ds(r0, TM), :] = kv[:, KV_WIDTH:].astype(BF16)
            return carry
        lax.fori_loop(0, s // TM, prep, 0)

    qr = (_rope(q_ref[0], cq_ref[...], sq_ref[...]) * scale).astype(BF16)
    sink = sink_ref[...]

    def head_slices(h):
        kvh = h // (GROUP_HEADS // KV_HEADS)
        return slice(h * HEAD_DIM, (h + 1) * HEAD_DIM), slice(kvh * HEAD_DIM, (kvh + 1) * HEAD_DIM)

    def nt_dot(a, bmat):
        return lax.dot_general(a, bmat, (((1,), (1,)), ((), ())), preferred_element_type=F32)

    def ctx_tile():
        outs = []
        for h in range(GROUP_HEADS):
            hs, ks = head_slices(h)
            sc = nt_dot(qr[:, hs], k_s[0:n_ctx, ks])
            sk_h = sink[:, h * HEAD_DIM:h * HEAD_DIM + 1]
            m = jnp.maximum(jnp.max(sc, axis=-1, keepdims=True), sk_h)
            pr = jnp.exp(sc - m)
            l = jnp.sum(pr, axis=-1, keepdims=True) + jnp.exp(sk_h - m)
            outs.append(_dot(pr.astype(BF16), v_s[0:n_ctx, ks]) / l)
        o_ref[0] = jnp.concatenate(outs, axis=-1).astype(o_ref.dtype)

    def latent_tile():
        n = t + t0 - n_ctx // qb
        start = pl.multiple_of(jnp.minimum(n_ctx - qb + qb * n, s - band_w), qb)
        r = lax.broadcasted_iota(jnp.int32, (qb, band_w), 0)
        c = lax.broadcasted_iota(jnp.int32, (qb, band_w), 1)
        k_pos = start - n_ctx + c
        dist = n * qb + r - k_pos
        valid = (jnp.abs(dist) <= WINDOW) & (k_pos >= 0)
        kb = k_s[pl.ds(start, band_w), :]
        vb = v_s[pl.ds(start, band_w), :]
        outs = []
        for h in range(GROUP_HEADS):
            hs, ks = head_slices(h)
            s_band = jnp.where(valid, nt_dot(qr[:, hs], kb[:, ks]), NEG)
            s_ctx = nt_dot(qr[:, hs], k_s[0:n_ctx, ks])
            sk_h = sink[:, h * HEAD_DIM:h * HEAD_DIM + 1]
            m = jnp.maximum(jnp.maximum(jnp.max(s_band, axis=-1, keepdims=True),
                                        jnp.max(s_ctx, axis=-1, keepdims=True)), sk_h)
            p_band = jnp.exp(s_band - m)
            p_ctx = jnp.exp(s_ctx - m)
            l = (jnp.sum(p_band, axis=-1, keepdims=True) + jnp.sum(p_ctx, axis=-1, keepdims=True)
                 + jnp.exp(sk_h - m))
            o = _dot(p_band.astype(BF16), vb[:, ks]) + _dot(p_ctx.astype(BF16), v_s[0:n_ctx, ks])
            outs.append(o / l)
        o_ref[0] = jnp.concatenate(outs, axis=-1).astype(o_ref.dtype)

    if t0 == 0:
        pl.when(t < n_ctx // qb)(ctx_tile)
        pl.when(t >= n_ctx // qb)(latent_tile)
    else:
        latent_tile()


def _wattn(p, cos256, sin256, sink, n_ctx, with_ctx):
    b, s, _ = p.shape
    gw = GROUP_WIDTH
    qb = Q_BLOCK
    assert n_ctx >= qb and s - n_ctx >= 3 * qb and n_ctx % qb == 0
    t0 = 0 if with_ctx else n_ctx // qb
    nt = s // qb - t0
    sink_row = jnp.repeat(sink, HEAD_DIM).reshape(1, gw)
    ck, sk = cos256[:, :KV_WIDTH], sin256[:, :KV_WIDTH]
    return pl.pallas_call(
        functools.partial(_wattn_kernel, n_ctx=n_ctx, t0=t0),
        grid=(b, nt),
        in_specs=[pl.BlockSpec((1, qb, gw), lambda i, t: (i, t + t0, COL_SWA_Q // gw)),
                  pl.BlockSpec((1, s, gw), lambda i, t: (i, 0, COL_SWA_KV // gw)),
                  pl.BlockSpec((qb, gw), lambda i, t: (t + t0, 0)),
                  pl.BlockSpec((qb, gw), lambda i, t: (t + t0, 0)),
                  _full_spec(ck), _full_spec(sk), _full_spec(sink_row)],
        out_specs=pl.BlockSpec((1, qb, gw), lambda i, t: (i, t, 0)),
        out_shape=jax.ShapeDtypeStruct((b, nt * qb, gw), BF16),
        scratch_shapes=[pltpu.VMEM((s, KV_WIDTH), BF16), pltpu.VMEM((s, KV_WIDTH), BF16)],
        compiler_params=_cparams(2),
        name="window_gqa",
    )(p, p, cos256, sin256, ck, sk, sink_row)


def _outproj_kernel(a_ref, b_ref, c_ref, d_ref, x_ref, mod_ref, gpost_ref, gpre_ref, w_ref, xo_ref, h_ref):
    gw = GROUP_WIDTH
    o = (_dot(a_ref[0], w_ref[0:gw]) + _dot(b_ref[0], w_ref[gw:2 * gw])
         + _dot(c_ref[0], w_ref[2 * gw:3 * gw]) + _dot(d_ref[0], w_ref[3 * gw:4 * gw]))
    mod = mod_ref[0, 0]
    x = x_ref[0] + mod[2:3] * _rms(o, gpost_ref[...])
    xo_ref[0] = x
    h_ref[0] = (_rms(x, gpre_ref[...]) * (1.0 + mod[4:5]) + mod[3:4]).astype(h_ref.dtype)


def _outproj(mix, mix_t0, xs, modt, g_post, g_pre, w_bf16, t0):
    b, s, d = xs.shape
    gw = GROUP_WIDTH
    nt = s // TM - t0

    def mix_spec(off):
        return pl.BlockSpec((1, TM, gw), lambda i, t: (i, t + off, 0))

    return pl.pallas_call(
        _outproj_kernel,
        grid=(b, nt),
        in_specs=[mix_spec(o) for o in mix_t0]
                 + [pl.BlockSpec((1, TM, d), lambda i, t: (i, t + t0, 0)),
                    pl.BlockSpec((1, 1, N_MOD, d), lambda i, t: (i, jnp.minimum(t + t0, 1), 0, 0)),
                    pl.BlockSpec((1, d), lambda i, t: (0, 0)),
                    pl.BlockSpec((1, d), lambda i, t: (0, 0)),
                    pl.BlockSpec((4 * gw, d), lambda i, t: (0, 0))],
        out_specs=[pl.BlockSpec((1, TM, d), lambda i, t: (i, t, 0)),
                   pl.BlockSpec((1, TM, d), lambda i, t: (i, t, 0))],
        out_shape=[jax.ShapeDtypeStruct((b, nt * TM, d), F32),
                   jax.ShapeDtypeStruct((b, nt * TM, d), BF16)],
        compiler_params=_cparams(2),
        name="outproj",
    )(*mix, xs, modt, g_post.reshape(1, d), g_pre.reshape(1, d), w_bf16)


def _ffn_kernel(h_ref, hp_ref, hn_ref, x_ref, mod_ref, gain_ref, wup_ref, cw_ref, cb_ref, wdn_ref, o_ref,
                *, ctx_tiles, n_tiles, d_ff, fc):
    t = pl.program_id(1)
    zero_prev = t == 0
    zero_next = t == n_tiles - 1
    if ctx_tiles:
        zero_prev = zero_prev | (t == ctx_tiles)
        zero_next = zero_next | (t == ctx_tiles - 1)
    h = h_ref[0]
    halo = jnp.concatenate([hp_ref[0], hn_ref[0]], axis=0)
    row = lax.broadcasted_iota(jnp.int32, (TM, fc), 0)
    acc = jnp.zeros((TM, x_ref.shape[2]), F32)
    for c0 in range(0, d_ff, fc):
        act = None
        for base in (c0, d_ff + c0):
            w = wup_ref[:, base:base + fc]
            u = _dot(h, w)
            uh = _dot(halo, w)
            prev_row = jnp.where(zero_prev, 0.0, uh[15:16])
            next_row = jnp.where(zero_next, 0.0, uh[16:17])
            u_dn = jnp.where(row == 0, prev_row, pltpu.roll(u, 1, 0))
            u_up = jnp.where(row == TM - 1, next_row, pltpu.roll(u, TM - 1, 0))
            y = (u_dn * cw_ref[0:1, base:base + fc] + u * cw_ref[1:2, base:base + fc]
                 + u_up * cw_ref[2:3, base:base + fc] + cb_ref[0:1, base:base + fc])
            act = _silu(y) if act is None else act * y
        acc = acc + _dot(act.astype(BF16), wdn_ref[c0:c0 + fc, :])
    mod = mod_ref[0, 0]
    o_ref[0] = x_ref[0] + mod[5:6] * _rms(acc, gain_ref[...])


def _ffn(h2, x_mid, modt, gain, wup_bf16, conv_w, conv_b, wdn_bf16, ctx_tiles):
    b, s, d = x_mid.shape
    d_ff = wdn_bf16.shape[0]
    fc = d_ff // 2
    assert fc % 128 == 0
    nt = s // TM
    hb = TM // 16
    mod_off = 0 if ctx_tiles else 1
    single = dict(pipeline_mode=pl.Buffered(1))
    return pl.pallas_call(
        functools.partial(_ffn_kernel, ctx_tiles=ctx_tiles, n_tiles=nt, d_ff=d_ff, fc=fc),
        grid=(b, nt),
        in_specs=[pl.BlockSpec((1, TM, d), lambda i, t: (i, t, 0)),
                  pl.BlockSpec((1, 16, d), lambda i, t: (i, jnp.maximum(t * hb - 1, 0), 0)),
                  pl.BlockSpec((1, 16, d), lambda i, t: (i, jnp.minimum((t + 1) * hb, nt * hb - 1), 0)),
                  pl.BlockSpec((1, TM, d), lambda i, t: (i, t, 0)),
                  pl.BlockSpec((1, 1, N_MOD, d), lambda i, t: (i, jnp.minimum(t + mod_off, 1), 0, 0)),
                  pl.BlockSpec((1, d), lambda i, t: (0, 0)),
                  pl.BlockSpec((d, 2 * d_ff), lambda i, t: (0, 0), **single),
                  pl.BlockSpec((3, 2 * d_ff), lambda i, t: (0, 0)),
                  pl.BlockSpec((1, 2 * d_ff), lambda i, t: (0, 0)),
                  pl.BlockSpec((d_ff, d), lambda i, t: (0, 0), **single)],
        out_specs=pl.BlockSpec((1, TM, d), lambda i, t: (i, t, 0)),
        out_shape=jax.ShapeDtypeStruct((b, s, d), F32),
        compiler_params=_cparams(2),
        name="conv_ffn",
    )(h2, h2, h2, x_mid, modt, gain.reshape(1, d), wup_bf16, conv_w, conv_b.reshape(1, -1), wdn_bf16)


def _rope_tables(n_ctx, length):
    rows = length // GRID_W
    row = jnp.repeat(jnp.arange(rows), GRID_W).astype(F32)
    col = jnp.tile(jnp.arange(GRID_W), rows).astype(F32)
    n_freq = HEAD_DIM // 4
    inv_freq = 1.0 / (ROPE_THETA ** (jnp.arange(n_freq, dtype=F32) / n_freq))
    ang = jnp.concatenate([row[:, None] * inv_freq, col[:, None] * inv_freq], axis=-1)
    cos, sin = jnp.cos(ang), jnp.sin(ang)
    cos_full = jnp.repeat(cos, 2, axis=-1)
    sin_signed = jnp.stack([-sin, sin], axis=-1).reshape(length, HEAD_DIM)
    cos_full = jnp.concatenate([jnp.ones((n_ctx, HEAD_DIM), F32), cos_full], axis=0)
    sin_signed = jnp.concatenate([jnp.zeros((n_ctx, HEAD_DIM), F32), sin_signed], axis=0)
    return jnp.tile(cos_full, (1, GROUP_HEADS)), jnp.tile(sin_signed, (1, GROUP_HEADS))


def _permute_w_in(w):
    pad = jnp.zeros(w.shape[:-1] + (P_WIDTH - w.shape[-1],), w.dtype)
    return jnp.concatenate([w[..., 0:1024], w[..., 1040:], w[..., 1024:1040], pad], axis=-1)


def kernel(x, c, ctx, c_ctx, ada_w, ada_b, norm_pre_mix, norm_post_mix, norm_pre_ffn, norm_post_ffn, w_in, w_out, gdn_conv_w, gdn_a_log, gdn_dt_bias, gdn_norm, attn_q_norm, attn_k_norm, swa_sink, hgrn_lb_raw, hgrn_norm, ffn_w_up, ffn_conv_w, ffn_conv_b, ffn_w_down):
    bsz, length, d = x.shape
    n_ctx = ctx.shape[1]
    depth = ada_w.shape[0]
    assert n_ctx == TM and length % TM == 0 and w_in.shape[-1] == 3344
    ctx_tiles = n_ctx // TM

    xs = jnp.concatenate([ctx, x], axis=1)
    rows = -(-(bsz + 1) // 8) * 8
    cstack = jnp.concatenate([c, c_ctx[None], jnp.zeros((rows - bsz - 1, d), F32)], axis=0)
    mod_all = _ada_mod(cstack, ada_w, ada_b)
    cos256, sin256 = _rope_tables(n_ctx, length)
    w_in_p = _permute_w_in(w_in).astype(BF16)
    w_out_b = w_out.astype(BF16)
    w_up_b = ffn_w_up.astype(BF16)
    w_dn_b = ffn_w_down.astype(BF16)

    for layer in range(depth):
        with_ctx = layer < depth - 1
        mod_l = mod_all[layer, :bsz].reshape(bsz, N_MOD, d)
        mod_c = jnp.broadcast_to(mod_all[layer, bsz].reshape(1, N_MOD, d), (bsz, N_MOD, d))
        modt = jnp.stack([mod_c, mod_l], axis=1)
        p = _inproj(xs, modt, norm_pre_mix[layer], w_in_p[layer])
        mix_a = _gdn(p, gdn_conv_w[layer], gdn_a_log[layer], gdn_dt_bias[layer], gdn_norm[layer], n_ctx)
        mix_b = _gattn(p, cos256, sin256, attn_q_norm[layer], attn_k_norm[layer], n_ctx, with_ctx)
        mix_c = _wattn(p, cos256, sin256, swa_sink[layer], n_ctx, with_ctx)
        mix_d = _hgrn(p, hgrn_lb_raw, hgrn_norm[layer], n_ctx, layer)
        t0 = 0 if with_ctx else ctx_tiles
        x_mid, h2 = _outproj((mix_a, mix_b, mix_c, mix_d), (t0, 0, 0, t0), xs, modt,
                             norm_post_mix[layer], norm_pre_ffn[layer], w_out_b[layer], t0)
        xs = _ffn(h2, x_mid, modt, norm_post_ffn[layer], w_up_b[layer], ffn_conv_w[layer],
                  ffn_conv_b[layer], w_dn_b[layer], ctx_tiles if with_ctx else 0)
    return xs
```

```python
import functools

import numpy as np
import jax
import jax.numpy as jnp
from jax import lax
from jax.experimental import pallas as pl
from jax.experimental.pallas import tpu as pltpu

F32 = jnp.float32
BF16 = jnp.bfloat16

HEAD_DIM = 64
GROUP_HEADS = 4
GROUP_WIDTH = GROUP_HEADS * HEAD_DIM
KV_HEADS = 2
KV_WIDTH = KV_HEADS * HEAD_DIM
GRID_W = 64
GDN_CHUNK = 64
HGRN_CHUNK = 16
Q_BLOCK = 128
WINDOW = 128
ROPE_THETA = 10000.0
N_MOD = 6
EPS = 1e-6
NEG = -1e30
TM = 256
GDN_PER_TILE = TM // GDN_CHUNK
HGRN_PER_TILE = TM // HGRN_CHUNK
ROW_PAD = 16
VMEM_LIMIT = 56 * 1024 * 1024

COL_GDN_QKV = 0
COL_GDN_Z = 768
COL_ATT_Q = 1024
COL_ATT_KV = 1280
COL_SWA_Q = 1536
COL_SWA_KV = 1792
COL_HG_Q = 2048
COL_HG_F = 2304
COL_HG_I = 2816
COL_HG_G = 3072
COL_GDN_BA = 3328
P_WIDTH = 3456


def _cparams(n_axes):
    return pltpu.CompilerParams(dimension_semantics=("arbitrary",) * n_axes,
                                vmem_limit_bytes=VMEM_LIMIT)


def _dot(a, b):
    return jnp.dot(a, b, preferred_element_type=F32)


def _dot_nt(a, b):
    return lax.dot_general(a, b, (((1,), (1,)), ((), ())), preferred_element_type=F32)


def _dot_tn(a, b):
    return lax.dot_general(a, b, (((0,), (0,)), ((), ())), preferred_element_type=F32)


def _split(a, parts):
    out = []
    r = a
    for i in range(parts):
        p = r.astype(BF16)
        out.append(p)
        if i + 1 < parts:
            r = r - p.astype(F32)
    return out


def _mm_x01(a, b01, parts=3):
    m = a.shape[0]
    r = _dot(jnp.concatenate(_split(a, parts), axis=0), b01)
    return sum(r[i * m:(i + 1) * m] for i in range(parts))


def _mm_01x(a01, b, parts=3):
    n = b.shape[1]
    r = _dot(a01, jnp.concatenate(_split(b, parts), axis=1))
    return sum(r[:, i * n:(i + 1) * n] for i in range(parts))


def _mm_hi(a, b):
    ah, al = _split(a, 2)
    bh, bl = _split(b, 2)
    return _dot(ah, bh) + _dot(al, bh) + _dot(ah, bl)


def _silu(x):
    return x * jax.nn.sigmoid(x)


def _softplus(x):
    return jnp.maximum(x, 0.0) + jnp.log1p(jnp.exp(-jnp.abs(x)))


def _head_sumsq(x, ones_bd):
    return _mm_x01(x * x, ones_bd, parts=2)


def _head_rms(x, ones_bd, gain):
    return x * lax.rsqrt(_head_sumsq(x, ones_bd) * (1.0 / HEAD_DIM) + EPS) * gain


def _rope(x, cos, sin_signed):
    w = x.shape[-1]
    lane = lax.broadcasted_iota(jnp.int32, x.shape, x.ndim - 1)
    nxt = pltpu.roll(x, w - 1, x.ndim - 1)
    prv = pltpu.roll(x, 1, x.ndim - 1)
    swapped = jnp.where((lane & 1) == 0, nxt, prv)
    return x * cos + swapped * sin_signed


def _stack_heads(x_bf16, hm_bf16):
    return jnp.concatenate([x_bf16] * GROUP_HEADS, axis=0) * hm_bf16


def _mm_hi_heads(a, bs, hm_bf16):
    c = a.shape[0]
    ah, al = _split(a, 2)
    lhs = jnp.concatenate([ah, al], axis=0)
    outs = []
    for b in bs:
        bh, bl = _split(b, 2)
        r = _dot(lhs, _stack_heads(bh, hm_bf16))
        outs.append(r[0:c] + r[c:2 * c] + _dot(ah, _stack_heads(bl, hm_bf16)))
    return outs


def _block_diag_ones(n, blk):
    i = np.arange(n)
    return (i[:, None] // blk == i[None, :] // blk).astype(np.float32)


def _gdn_consts():
    cs = GDN_CHUNK
    hm = _block_diag_ones(GROUP_WIDTH, HEAD_DIM)
    r = np.arange(TM)
    same = r[:, None] // cs == r[None, :] // cs
    tri_rows = np.stack([same & (r[None, :] <= r[:, None]), same & (r[None, :] >= r[:, None])])
    tri_ones = np.concatenate([tri_rows, np.broadcast_to(same, (2, TM, TM))], axis=1).astype(np.float32)
    tri_lanes = np.stack([same & (r[:, None] <= r[None, :]), same & (r[:, None] >= r[None, :])]).astype(np.float32)
    ea = np.zeros((2, 128, GROUP_WIDTH), np.float32)
    eb = np.zeros((2, 128, GROUP_WIDTH), np.float32)
    for d in range(2):
        for h in range(GROUP_HEADS):
            eb[d, d * 4 + h, h * 64:(h + 1) * 64] = 1.0
            ea[d, 8 + d * 4 + h, h * 64:(h + 1) * 64] = 1.0
    i = np.arange(cs)[:, None]
    j = np.arange(GROUP_WIDTH)[None, :] % cs
    incl = np.stack([j <= i, j >= i]).astype(np.float32)
    strict = np.stack([j < i, j > i]).astype(np.float32)
    lvl = np.stack([((i >> (l + 1)) == (j >> (l + 1))) & ((i >> l) != (j >> l)) for l in range(6)]).astype(np.float32)
    eye = (i == j).astype(np.float32)
    last = np.stack([same & (r[:, None] % cs == e) for e in (cs - 1, 0)]).astype(np.float32)
    return [jnp.asarray(hm, BF16), jnp.asarray(tri_ones, BF16), jnp.asarray(tri_lanes, BF16),
            jnp.asarray(ea, BF16), jnp.asarray(eb, BF16), jnp.asarray(incl, F32), jnp.asarray(strict, F32),
            jnp.asarray(lvl, F32), jnp.asarray(eye, F32), jnp.asarray(last, BF16)]


def _hgrn_consts():
    r = np.arange(TM)
    same = r[:, None] // HGRN_CHUNK == r[None, :] // HGRN_CHUNK
    tri = np.stack([same & (r[None, :] <= r[:, None]), same & (r[None, :] >= r[:, None])])
    tri_ones = np.concatenate([tri, np.broadcast_to(same, (2, TM, TM))], axis=1).astype(np.float32)
    chunk_ind = (np.arange(HGRN_PER_TILE)[:, None] == r[None, :] // HGRN_CHUNK).astype(np.float32)
    return [jnp.asarray(_block_diag_ones(TM, HEAD_DIM), BF16), jnp.asarray(tri_ones, BF16),
            jnp.asarray(chunk_ind, BF16)]


def _full_spec(arr):
    nd = arr.ndim
    return pl.BlockSpec(arr.shape, lambda *_: (0,) * nd)


def _flip_tile(t, ctx_tiles, n_tiles):
    return jnp.where(t < ctx_tiles, ctx_tiles - 1 - t, n_tiles - 1 + ctx_tiles - t)


def _ada_kernel(c_ref, w_ref, b_ref, o_ref):
    act = _silu(c_ref[...])
    o_ref[0] = _mm_hi(act, w_ref[0]) + b_ref[0]


def _ada_mod(cstack, ada_w, ada_b):
    depth, d, n = ada_w.shape
    tn = 1536
    rows = cstack.shape[0]
    return pl.pallas_call(
        _ada_kernel,
        grid=(depth, n // tn),
        in_specs=[pl.BlockSpec((rows, d), lambda l, j: (0, 0)),
                  pl.BlockSpec((1, d, tn), lambda l, j: (l, 0, j)),
                  pl.BlockSpec((1, 1, tn), lambda l, j: (l, 0, j))],
        out_specs=pl.BlockSpec((1, rows, tn), lambda l, j: (l, 0, j)),
        out_shape=jax.ShapeDtypeStruct((depth, rows, n), F32),
        compiler_params=_cparams(2),
        name="ada_mod",
    )(cstack, ada_w, ada_b.reshape(depth, 1, n))


def _rms(x, gain):
    return x * lax.rsqrt(jnp.mean(x * x, axis=-1, keepdims=True) + EPS) * gain


def _inproj_kernel(x_ref, mod_ref, gain_ref, w_ref, o_ref):
    mod = mod_ref[0, 0]
    h = _rms(x_ref[0], gain_ref[...]) * (1.0 + mod[1:2]) + mod[0:1]
    o_ref[0] = _dot(h.astype(BF16), w_ref[...])


def _inproj(xs, modt, gain, w_bf16):
    b, s, d = xs.shape
    n = w_bf16.shape[1]
    return pl.pallas_call(
        _inproj_kernel,
        grid=(b, s // TM),
        in_specs=[pl.BlockSpec((1, TM, d), lambda i, t: (i, t, 0)),
                  pl.BlockSpec((1, 1, N_MOD, d), lambda i, t: (i, jnp.minimum(t, 1), 0, 0)),
                  pl.BlockSpec((1, d), lambda i, t: (0, 0)),
                  pl.BlockSpec((d, n), lambda i, t: (0, 0))],
        out_specs=pl.BlockSpec((1, TM, n), lambda i, t: (i, t, 0)),
        out_shape=jax.ShapeDtypeStruct((b, s, n), F32),
        compiler_params=_cparams(2),
        name="inproj",
    )(xs, modt, gain.reshape(1, d), w_bf16)


def _gdn_prep_kernel(qkv_ref, prev_ref, next_ref, ba_ref, arow_ref, cw_ref, alog_ref, dt_ref, alog_row_ref,
                     dt_row_ref, hmb_ref, trio_ref, tril_ref, ea_ref, eb_ref, incl_ref, strict_ref, lvl_ref,
                     eye_ref, last_ref, u_ref, w_ref, at_ref, qd_ref, kd_ref, gl_ref, *, ctx_tiles, n_tiles):
    gw = GROUP_WIDTH
    cs = GDN_CHUNK
    t = pl.program_id(1)
    hmb = hmb_ref[...]

    x = qkv_ref[0]
    zero_prev = (t == 0) | (t == ctx_tiles)
    zero_next = (t == ctx_tiles - 1) | (t == n_tiles - 1)
    prev_row = jnp.where(zero_prev, 0.0, prev_ref[0, 7:8])
    next_row = jnp.where(zero_next, 0.0, next_ref[0, 0:1])
    row = lax.broadcasted_iota(jnp.int32, x.shape, 0)
    x_dn = jnp.where(row == 0, prev_row, pltpu.roll(x, 1, 0))
    x_up = jnp.where(row == TM - 1, next_row, pltpu.roll(x, TM - 1, 0))
    y = _silu(x_dn * cw_ref[0:1] + x * cw_ref[1:2] + x_up * cw_ref[2:3])
    q, k, v = y[:, 0:gw], y[:, gw:2 * gw], y[:, 2 * gw:3 * gw]
    q = q * lax.rsqrt(_head_sumsq(q, hmb) + EPS) * (HEAD_DIM ** -0.5)
    k = k * lax.rsqrt(_head_sumsq(k, hmb) + EPS)

    ba = ba_ref[0]
    beta_c = jax.nn.sigmoid(ba)
    g_c = -jnp.exp(alog_ref[...]) * _softplus(ba + dt_ref[...])

    for d in range(2):
        cums = _mm_01x(trio_ref[d], g_c)
        nat = _mm_x01(cums, ea_ref[d])
        gn_gc, gn_tot = nat[0:TM], nat[TM:]
        gn_beta = _mm_x01(beta_c, eb_ref[d])
        g_row = -jnp.exp(alog_row_ref[d:d + 1]) * _softplus(arow_ref[0, d, 0] + dt_row_ref[d:d + 1])
        gc_row = _mm_x01(g_row, tril_ref[d])
        e_gc = jnp.exp(gn_gc)
        qd_ref[0, d] = (q * e_gc).astype(BF16)
        kd_ref[0, d] = (k * jnp.exp(gn_tot - gn_gc)).astype(BF16)
        vb = v * gn_beta
        kbg = k * gn_beta * e_gc
        gl_ref[0, d, 0] = jnp.exp(_mm_x01(gc_row, last_ref[d]))
        for c in range(GDN_PER_TILE):
            sl = slice(c * cs, (c + 1) * cs)
            decay = jnp.exp(jnp.where(incl_ref[d] > 0.5, gn_gc[sl] - gc_row[c:c + 1], NEG))
            kst = _stack_heads(k[sl].astype(BF16), hmb)
            qk_kk = _dot_nt(jnp.concatenate([q[sl], k[sl]], axis=0).astype(BF16), kst)
            mmat = strict_ref[d] * gn_beta[sl] * qk_kk[cs:] * decay
            tmat = eye_ref[...] - mmat * lvl_ref[0]
            for lv in range(1, 6):
                (tc,) = _mm_hi_heads(tmat, [mmat * lvl_ref[lv]], hmb)
                (tt,) = _mm_hi_heads(tc, [tmat], hmb)
                tmat = tmat - tt
            u, w = _mm_hi_heads(tmat, [vb[sl], kbg[sl]], hmb)
            u_ref[0, d, sl, :] = u
            w_ref[0, d, sl, :] = w.astype(BF16)
            at_ref[0, d, sl, :] = (qk_kk[0:cs] * decay).astype(BF16)


def _gdn_scan_kernel(uf, wf, af, qf, kf, gf, ub, wb, ab, qb, kb, gb, hm_ref, hmb_ref, of_ref, ob_ref, st_s):
    cs = GDN_CHUNK

    @pl.when(pl.program_id(1) == 0)
    def _():
        st_s[...] = jnp.zeros_like(st_s)

    def chunk(refs, d, c, o_ref):
        u_ref, w_ref, a_ref, q_ref, k_ref, g_ref = refs
        sl = slice(c * cs, (c + 1) * cs)
        state = st_s[d]
        r = _dot(jnp.concatenate([w_ref[0, 0, sl, :], q_ref[0, 0, sl, :]], axis=0), state.astype(BF16))
        v_new = (u_ref[0, 0, sl, :] - r[0:cs]).astype(BF16)
        o_ref[0, sl, :] = r[cs:] + _dot(a_ref[0, 0, sl, :], _stack_heads(v_new, hmb_ref[...]))
        st_s[d] = state * g_ref[0, 0, 0, c:c + 1, :] + hm_ref[...] * _dot_tn(k_ref[0, 0, sl, :], v_new)

    for c in range(GDN_PER_TILE):
        chunk((uf, wf, af, qf, kf, gf), 0, c, of_ref)
        chunk((ub, wb, ab, qb, kb, gb), 1, GDN_PER_TILE - 1 - c, ob_ref)


def _gdn(p, conv_w, a_log, dt_bias, n_ctx):
    b, s, _ = p.shape
    gw = GROUP_WIDTH
    nt = s // TM
    ctx_tiles = n_ctx // TM
    consts = _gdn_consts()
    alog_c = jnp.zeros((1, 128), F32).at[0, 8:16].set(a_log.reshape(-1))
    dt_c = jnp.zeros((1, 128), F32).at[0, 8:16].set(dt_bias.reshape(-1))
    alog_row = jnp.repeat(a_log, GDN_CHUNK, axis=1)
    dt_row = jnp.repeat(dt_bias, GDN_CHUNK, axis=1)
    a_raw = p[:, :, COL_GDN_BA + 8:COL_GDN_BA + 16].reshape(b, nt, GDN_PER_TILE, GDN_CHUNK, 2, GROUP_HEADS)
    arow = a_raw.transpose(0, 4, 1, 2, 5, 3).reshape(b, 2, nt, GDN_PER_TILE, gw)
    arow = jnp.pad(arow, ((0, 0), (0, 0), (0, 0), (0, ROW_PAD - GDN_PER_TILE), (0, 0)))
    hb = TM // 8
    dir_spec = pl.BlockSpec((1, 2, TM, gw), lambda i, t: (i, 0, t, 0))
    u, w, at, qd, kd, gl = pl.pallas_call(
        functools.partial(_gdn_prep_kernel, ctx_tiles=ctx_tiles, n_tiles=nt),
        grid=(b, nt),
        in_specs=[pl.BlockSpec((1, TM, 3 * gw), lambda i, t: (i, t, COL_GDN_QKV // (3 * gw))),
                  pl.BlockSpec((1, 8, 3 * gw), lambda i, t: (i, jnp.maximum(t * hb - 1, 0), 0)),
                  pl.BlockSpec((1, 8, 3 * gw), lambda i, t: (i, jnp.minimum((t + 1) * hb, nt * hb - 1), 0)),
                  pl.BlockSpec((1, TM, 128), lambda i, t: (i, t, COL_GDN_BA // 128)),
                  pl.BlockSpec((1, 2, 1, ROW_PAD, gw), lambda i, t: (i, 0, t, 0, 0)),
                  _full_spec(conv_w), _full_spec(alog_c), _full_spec(dt_c), _full_spec(alog_row),
                  _full_spec(dt_row)] + [_full_spec(c) for c in consts],
        out_specs=[dir_spec] * 5 + [pl.BlockSpec((1, 2, 1, ROW_PAD, gw), lambda i, t: (i, 0, t, 0, 0))],
        out_shape=[jax.ShapeDtypeStruct((b, 2, s, gw), F32)] + [jax.ShapeDtypeStruct((b, 2, s, gw), BF16)] * 4
                  + [jax.ShapeDtypeStruct((b, 2, nt, ROW_PAD, gw), F32)],
        compiler_params=_cparams(2),
        name="gdn_prep",
    )(p, p, p, p, arow, conv_w, alog_c, dt_c, alog_row, dt_row, *consts)

    def fwd_spec():
        return pl.BlockSpec((1, 1, TM, gw), lambda i, t: (i, 0, t, 0))

    def bwd_spec():
        return pl.BlockSpec((1, 1, TM, gw), lambda i, t: (i, 1, _flip_tile(t, ctx_tiles, nt), 0))

    gf_spec = pl.BlockSpec((1, 1, 1, ROW_PAD, gw), lambda i, t: (i, 0, t, 0, 0))
    gb_spec = pl.BlockSpec((1, 1, 1, ROW_PAD, gw), lambda i, t: (i, 1, _flip_tile(t, ctx_tiles, nt), 0, 0))
    hm = jnp.asarray(_block_diag_ones(gw, HEAD_DIM), F32)
    hmb = hm.astype(BF16)
    return pl.pallas_call(
        _gdn_scan_kernel,
        grid=(b, nt),
        in_specs=[fwd_spec()] * 5 + [gf_spec] + [bwd_spec()] * 5 + [gb_spec] + [_full_spec(hm), _full_spec(hmb)],
        out_specs=[pl.BlockSpec((1, TM, gw), lambda i, t: (i, t, 0)),
                   pl.BlockSpec((1, TM, gw), lambda i, t: (i, _flip_tile(t, ctx_tiles, nt), 0))],
        out_shape=[jax.ShapeDtypeStruct((b, s, gw), F32)] * 2,
        scratch_shapes=[pltpu.VMEM((2, gw, gw), F32)],
        compiler_params=_cparams(2),
        name="gdn_scan",
    )(u, w, at, qd, kd, gl, u, w, at, qd, kd, gl, hm, hmb)


def _hgrn_prep_kernel(q_ref, f0_ref, f1_ref, i_ref, lbraw_ref, hmb_ref, trio_ref, cind_ref,
                      oin_ref, qd_ref, kd_ref, fl_ref, *, layer):
    depth = lbraw_ref.shape[0]
    raw = [lbraw_ref[l] for l in range(depth)]
    mx = functools.reduce(jnp.maximum, raw)
    ex = [jnp.exp(r - mx) for r in raw]
    lb_all = sum(ex[1:layer + 1], jnp.zeros_like(mx)) / sum(ex)

    q = _silu(q_ref[0])
    v = i_ref[0]
    pos = lax.broadcasted_iota(jnp.int32, (TM, GROUP_WIDTH), 0) % HGRN_CHUNK
    o_acc = jnp.zeros((TM, GROUP_WIDTH), F32)
    for d, f_ref in enumerate((f0_ref, f1_ref)):
        lb = lb_all[d:d + 1]
        z = f_ref[0]
        f = lb + (1.0 - lb) * jax.nn.sigmoid(z)
        logf = jnp.log(f)
        k = (1.0 - lb) * jax.nn.sigmoid(-z)
        cums = _mm_01x(trio_ref[d], logf)
        bc, tot = cums[0:TM], cums[TM:]
        qd_ref[0, d] = (q * jnp.exp(bc)).astype(BF16)
        kd_ref[0, d] = (k * jnp.exp(tot - bc)).astype(BF16)
        fl_ref[0, d, 0] = jnp.exp(_mm_01x(cind_ref[...], logf))
        for delta in range(HGRN_CHUNK):
            if d == 0:
                sh = delta
                ok = pos >= delta
            else:
                sh = (TM - delta) % TM
                ok = pos + delta <= HGRN_CHUNK - 1
            if delta == 0:
                k_sh, bc_sh, v_sh = k, bc, v
            else:
                k_sh = pltpu.roll(k, sh, 0)
                bc_sh = pltpu.roll(bc, sh, 0)
                v_sh = pltpu.roll(v, sh, 0)
            e = q * k_sh * jnp.exp(jnp.where(ok, bc - bc_sh, NEG))
            o_acc = o_acc + _mm_x01(e, hmb_ref[...], parts=2) * v_sh
    oin_ref[0] = o_acc


def _hgrn_scan_kernel(qf, kf, ff, vf, oin, qb, kb, fb, vb, hm_ref, of_ref, ob_ref, st_s):
    cs = HGRN_CHUNK

    @pl.when(pl.program_id(1) == 0)
    def _():
        st_s[...] = jnp.zeros_like(st_s)

    def chunk(q_ref, k_ref, f_ref, v_ref, d, c, base):
        sl = slice(c * cs, (c + 1) * cs)
        st = st_s[d]
        o = _dot_nt(q_ref[0, 0, sl, :], st.astype(BF16))
        upd = _dot_tn(v_ref[0, sl, :].astype(BF16), k_ref[0, 0, sl, :])
        st_s[d] = st * f_ref[0, 0, 0, c:c + 1, :] + hm_ref[...] * upd
        return o if base is None else o + base[0, sl, :]

    for c in range(HGRN_PER_TILE):
        sl = slice(c * cs, (c + 1) * cs)
        of_ref[0, sl, :] = chunk(qf, kf, ff, vf, 0, c, oin)
        cb = HGRN_PER_TILE - 1 - c
        ob_ref[0, cb * cs:(cb + 1) * cs, :] = chunk(qb, kb, fb, vb, 1, cb, None)


def _hgrn(p, lb_raw, n_ctx, layer):
    b, s, _ = p.shape
    gw = GROUP_WIDTH
    nt = s // TM
    ctx_tiles = n_ctx // TM
    consts = _hgrn_consts()

    def col(c):
        return pl.BlockSpec((1, TM, gw), lambda i, t: (i, t, c // gw))

    dir_spec = pl.BlockSpec((1, 2, TM, gw), lambda i, t: (i, 0, t, 0))
    oin, qd, kd, fl = pl.pallas_call(
        functools.partial(_hgrn_prep_kernel, layer=layer),
        grid=(b, nt),
        in_specs=[col(COL_HG_Q), col(COL_HG_F), col(COL_HG_F + gw), col(COL_HG_I), _full_spec(lb_raw)]
                 + [_full_spec(c) for c in consts],
        out_specs=[pl.BlockSpec((1, TM, gw), lambda i, t: (i, t, 0)), dir_spec, dir_spec,
                   pl.BlockSpec((1, 2, 1, HGRN_PER_TILE, gw), lambda i, t: (i, 0, t, 0, 0))],
        out_shape=[jax.ShapeDtypeStruct((b, s, gw), F32), jax.ShapeDtypeStruct((b, 2, s, gw), BF16),
                   jax.ShapeDtypeStruct((b, 2, s, gw), BF16),
                   jax.ShapeDtypeStruct((b, 2, nt, HGRN_PER_TILE, gw), F32)],
        compiler_params=_cparams(2),
        name="hgrn_prep",
    )(p, p, p, p, lb_raw, *consts)

    def flip(t):
        return _flip_tile(t, ctx_tiles, nt)

    hm = jnp.asarray(_block_diag_ones(gw, HEAD_DIM), F32)
    return pl.pallas_call(
        _hgrn_scan_kernel,
        grid=(b, nt),
        in_specs=[pl.BlockSpec((1, 1, TM, gw), lambda i, t: (i, 0, t, 0)),
                  pl.BlockSpec((1, 1, TM, gw), lambda i, t: (i, 0, t, 0)),
                  pl.BlockSpec((1, 1, 1, HGRN_PER_TILE, gw), lambda i, t: (i, 0, t, 0, 0)),
                  pl.BlockSpec((1, TM, gw), lambda i, t: (i, t, COL_HG_I // gw)),
                  pl.BlockSpec((1, TM, gw), lambda i, t: (i, t, 0)),
                  pl.BlockSpec((1, 1, TM, gw), lambda i, t: (i, 1, flip(t), 0)),
                  pl.BlockSpec((1, 1, TM, gw), lambda i, t: (i, 1, flip(t), 0)),
                  pl.BlockSpec((1, 1, 1, HGRN_PER_TILE, gw), lambda i, t: (i, 1, flip(t), 0, 0)),
                  pl.BlockSpec((1, TM, gw), lambda i, t: (i, flip(t), COL_HG_I // gw)),
                  _full_spec(hm)],
        out_specs=[pl.BlockSpec((1, TM, gw), lambda i, t: (i, t, 0)),
                   pl.BlockSpec((1, TM, gw), lambda i, t: (i, flip(t), 0))],
        out_shape=[jax.ShapeDtypeStruct((b, s, gw), F32)] * 2,
        scratch_shapes=[pltpu.VMEM((2, gw, gw), F32)],
        compiler_params=_cparams(2),
        name="hgrn_scan",
    )(qd, kd, fl, p, oin, qd, kd, fl, p, hm)


def _gattn_kernel(q_ref, kv_ref, cq_ref, sq_ref, ck_ref, sk_ref, qg_ref, kg_ref, hm4_ref, hm2_ref,
                  o_ref, k_s, v_s, *, n_ctx, t0):
    s = kv_ref.shape[1]
    t = pl.program_id(1)

    @pl.when(t == 0)
    def _():
        def prep(j, carry):
            r0 = pl.multiple_of(j * TM, TM)
            kv = kv_ref[0, pl.ds(r0, TM), :]
            kn = _head_rms(kv[:, 0:KV_WIDTH], hm2_ref[...], kg_ref[...])
            kn = _rope(kn, ck_ref[pl.ds(r0, TM), :], sk_ref[pl.ds(r0, TM), :])
            k_s[pl.ds(r0, TM), :] = kn.astype(BF16)
            v_s[pl.ds(r0, TM), :] = kv[:, KV_WIDTH:].astype(BF16)
            return carry
        lax.fori_loop(0, s // TM, prep, 0)

    qn = _head_rms(q_ref[0], hm4_ref[...], qg_ref[...])
    qn = (_rope(qn, cq_ref[...], sq_ref[...]) * (HEAD_DIM ** -0.5)).astype(BF16)

    def attend(nk):
        outs = []
        for h in range(GROUP_HEADS):
            kvh = h // (GROUP_HEADS // KV_HEADS)
            kh = k_s[0:nk, kvh * HEAD_DIM:(kvh + 1) * HEAD_DIM]
            vh = v_s[0:nk, kvh * HEAD_DIM:(kvh + 1) * HEAD_DIM]
            sc = _dot_nt(qn[:, h * HEAD_DIM:(h + 1) * HEAD_DIM], kh)
            m = jnp.max(sc, axis=-1, keepdims=True)
            pr = jnp.exp(sc - m)
            l = jnp.sum(pr, axis=-1, keepdims=True)
            outs.append(_dot(pr.astype(BF16), vh) / l)
        o_ref[0] = jnp.concatenate(outs, axis=-1).astype(o_ref.dtype)

    if t0 == 0:
        @pl.when(t < n_ctx // TM)
        def _():
            attend(n_ctx)

        @pl.when(t >= n_ctx // TM)
        def _():
            attend(s)
    else:
        attend(s)


def _gattn(p, cos256, sin256, q_gain, k_gain, n_ctx, with_ctx):
    b, s, _ = p.shape
    gw = GROUP_WIDTH
    t0 = 0 if with_ctx else n_ctx // TM
    nt = s // TM - t0
    qg = jnp.tile(q_gain, GROUP_HEADS).reshape(1, gw)
    kg = jnp.tile(k_gain, KV_HEADS).reshape(1, KV_WIDTH)
    hm4 = jnp.asarray(_block_diag_ones(gw, HEAD_DIM), BF16)
    hm2 = jnp.asarray(_block_diag_ones(KV_WIDTH, HEAD_DIM), BF16)
    ck, sk = cos256[:, :KV_WIDTH], sin256[:, :KV_WIDTH]
    return pl.pallas_call(
        functools.partial(_gattn_kernel, n_ctx=n_ctx, t0=t0),
        grid=(b, nt),
        in_specs=[pl.BlockSpec((1, TM, gw), lambda i, t: (i, t + t0, COL_ATT_Q // gw)),
                  pl.BlockSpec((1, s, gw), lambda i, t: (i, 0, COL_ATT_KV // gw)),
                  pl.BlockSpec((TM, gw), lambda i, t: (t + t0, 0)),
                  pl.BlockSpec((TM, gw), lambda i, t: (t + t0, 0)),
                  _full_spec(ck), _full_spec(sk), _full_spec(qg), _full_spec(kg),
                  _full_spec(hm4), _full_spec(hm2)],
        out_specs=pl.BlockSpec((1, TM, gw), lambda i, t: (i, t, 0)),
        out_shape=jax.ShapeDtypeStruct((b, nt * TM, gw), BF16),
        scratch_shapes=[pltpu.VMEM((s, KV_WIDTH), BF16), pltpu.VMEM((s, KV_WIDTH), BF16)],
        compiler_params=_cparams(2),
        name="global_gqa",
    )(p, p, cos256, sin256, ck, sk, qg, kg, hm4, hm2)


def _wattn_kernel(q_ref, kv_ref, cq_ref, sq_ref, ck_ref, sk_ref, sink_ref, o_ref, k_s, v_s, *, n_ctx, t0):
    s = kv_ref.shape[1]
    qb = Q_BLOCK
    band_w = 3 * qb
    t = pl.program_id(1)
    scale = HEAD_DIM ** -0.5

    @pl.when(t == 0)
    def _():
        def prep(j, carry):
            r0 = pl.multiple_of(j * TM, TM)
            kv = kv_ref[0, pl.ds(r0, TM), :]
            kr = _rope(kv[:, 0:KV_WIDTH], ck_ref[pl.ds(r0, TM), :], sk_ref[pl.ds(r0, TM), :])
            k_s[pl.ds(r0, TM), :] = kr.astype(BF16)
            v_s[pl.ds(r0, TM), :] = kv[:, KV_WIDTH:].astype(BF16)
            return carry
        lax.fori_loop(0, s // TM, prep, 0)

    qr = (_rope(q_ref[0], cq_ref[...], sq_ref[...]) * scale).astype(BF16)
    sink = sink_ref[...]

    def head_slices(h):
        kvh = h // (GROUP_HEADS // KV_HEADS)
        return slice(h * HEAD_DIM, (h + 1) * HEAD_DIM), slice(kvh * HEAD_DIM, (kvh + 1) * HEAD_DIM)

    def ctx_tile():
        outs = []
        for h in range(GROUP_HEADS):
            hs, ks = head_slices(h)
            sc = _dot_nt(qr[:, hs], k_s[0:n_ctx, ks])
            sk_h = sink[:, h * HEAD_DIM:h * HEAD_DIM + 1]
            m = jnp.maximum(jnp.max(sc, axis=-1, keepdims=True), sk_h)
            pr = jnp.exp(sc - m)
            l = jnp.sum(pr, axis=-1, keepdims=True) + jnp.exp(sk_h - m)
            outs.append(_dot(pr.astype(BF16), v_s[0:n_ctx, ks]) / l)
        o_ref[0] = jnp.concatenate(outs, axis=-1).astype(o_ref.dtype)

    def latent_tile():
        n = t + t0 - n_ctx // qb
        start = pl.multiple_of(jnp.minimum(n_ctx - qb + qb * n, s - band_w), qb)
        r = lax.broadcasted_iota(jnp.int32, (qb, band_w), 0)
        c = lax.broadcasted_iota(jnp.int32, (qb, band_w), 1)
        k_pos = start - n_ctx + c
        dist = n * qb + r - k_pos
        valid = (jnp.abs(dist) <= WINDOW) & (k_pos >= 0)
        kb = k_s[pl.ds(start, band_w), :]
        vb = v_s[pl.ds(start, band_w), :]
        outs = []
        for h in range(GROUP_HEADS):
            hs, ks = head_slices(h)
            s_band = jnp.where(valid, _dot_nt(qr[:, hs], kb[:, ks]), NEG)
            s_ctx = _dot_nt(qr[:, hs], k_s[0:n_ctx, ks])
            sk_h = sink[:, h * HEAD_DIM:h * HEAD_DIM + 1]
            m = jnp.maximum(jnp.maximum(jnp.max(s_band, axis=-1, keepdims=True),
                                        jnp.max(s_ctx, axis=-1, keepdims=True)), sk_h)
            p_band = jnp.exp(s_band - m)
            p_ctx = jnp.exp(s_ctx - m)
            l = (jnp.sum(p_band, axis=-1, keepdims=True) + jnp.sum(p_ctx, axis=-1, keepdims=True)
                 + jnp.exp(sk_h - m))
            o = _dot(p_band.astype(BF16), vb[:, ks]) + _dot(p_ctx.astype(BF16), v_s[0:n_ctx, ks])
            outs.append(o / l)
        o_ref[0] = jnp.concatenate(outs, axis=-1).astype(o_ref.dtype)

    if t0 == 0:
        pl.when(t < n_ctx // qb)(ctx_tile)
        pl.when(t >= n_ctx // qb)(latent_tile)
    else:
        latent_tile()


def _wattn(p, cos256, sin256, sink, n_ctx, with_ctx):
    b, s, _ = p.shape
    gw = GROUP_WIDTH
    qb = Q_BLOCK
    assert n_ctx >= qb and s - n_ctx >= 3 * qb and n_ctx % qb == 0
    t0 = 0 if with_ctx else n_ctx // qb
    nt = s // qb - t0
    sink_row = jnp.repeat(sink, HEAD_DIM).reshape(1, gw)
    ck, sk = cos256[:, :KV_WIDTH], sin256[:, :KV_WIDTH]
    return pl.pallas_call(
        functools.partial(_wattn_kernel, n_ctx=n_ctx, t0=t0),
        grid=(b, nt),
        in_specs=[pl.BlockSpec((1, qb, gw), lambda i, t: (i, t + t0, COL_SWA_Q // gw)),
                  pl.BlockSpec((1, s, gw), lambda i, t: (i, 0, COL_SWA_KV // gw)),
                  pl.BlockSpec((qb, gw), lambda i, t: (t + t0, 0)),
                  pl.BlockSpec((qb, gw), lambda i, t: (t + t0, 0)),
                  _full_spec(ck), _full_spec(sk), _full_spec(sink_row)],
        out_specs=pl.BlockSpec((1, qb, gw), lambda i, t: (i, t, 0)),
        out_shape=jax.ShapeDtypeStruct((b, nt * qb, gw), BF16),
        scratch_shapes=[pltpu.VMEM((s, KV_WIDTH), BF16), pltpu.VMEM((s, KV_WIDTH), BF16)],
        compiler_params=_cparams(2),
        name="window_gqa",
    )(p, p, cos256, sin256, ck, sk, sink_row)


def _outproj_kernel(gf_ref, gb_ref, gz_ref, b_ref, c_ref, hf_ref, hb_ref, hg_ref, x_ref, mod_ref,
                    ggain_ref, hgain_ref, gpost_ref, gpre_ref, hmb_ref, w_ref, xo_ref, h_ref):
    gw = GROUP_WIDTH
    hmb = hmb_ref[...]
    mix_a = _head_rms(gf_ref[0] + gb_ref[0], hmb, ggain_ref[...]) * _silu(gz_ref[0])
    mix_d = _head_rms(hf_ref[0] + hb_ref[0], hmb, hgain_ref[...]) * _silu(hg_ref[0])
    o = (_dot(mix_a.astype(BF16), w_ref[0:gw]) + _dot(b_ref[0], w_ref[gw:2 * gw])
         + _dot(c_ref[0], w_ref[2 * gw:3 * gw]) + _dot(mix_d.astype(BF16), w_ref[3 * gw:4 * gw]))
    mod = mod_ref[0, 0]
    x = x_ref[0] + mod[2:3] * _rms(o, gpost_ref[...])
    xo_ref[0] = x
    h_ref[0] = (_rms(x, gpre_ref[...]) * (1.0 + mod[4:5]) + mod[3:4]).astype(h_ref.dtype)


def _outproj(gdn_fb, attn_b, attn_c, hgrn_fb, p, xs, modt, gdn_gain, hgrn_gain, g_post, g_pre, w_bf16, t0):
    b, s, d = xs.shape
    gw = GROUP_WIDTH
    nt = s // TM - t0

    def full(col=0):
        return pl.BlockSpec((1, TM, gw), lambda i, t: (i, t + t0, col // gw))

    local = pl.BlockSpec((1, TM, gw), lambda i, t: (i, t, 0))
    row = pl.BlockSpec((1, d), lambda i, t: (0, 0))
    hrow = pl.BlockSpec((1, gw), lambda i, t: (0, 0))
    hmb = jnp.asarray(_block_diag_ones(gw, HEAD_DIM), BF16)
    return pl.pallas_call(
        _outproj_kernel,
        grid=(b, nt),
        in_specs=[full(), full(), full(COL_GDN_Z), local, local, full(), full(), full(COL_HG_G),
                  pl.BlockSpec((1, TM, d), lambda i, t: (i, t + t0, 0)),
                  pl.BlockSpec((1, 1, N_MOD, d), lambda i, t: (i, jnp.minimum(t + t0, 1), 0, 0)),
                  hrow, hrow, row, row, _full_spec(hmb),
                  pl.BlockSpec((4 * gw, d), lambda i, t: (0, 0))],
        out_specs=[pl.BlockSpec((1, TM, d), lambda i, t: (i, t, 0)),
                   pl.BlockSpec((1, TM, d), lambda i, t: (i, t, 0))],
        out_shape=[jax.ShapeDtypeStruct((b, nt * TM, d), F32),
                   jax.ShapeDtypeStruct((b, nt * TM, d), BF16)],
        compiler_params=_cparams(2),
        name="outproj",
    )(gdn_fb[0], gdn_fb[1], p, attn_b, attn_c, hgrn_fb[0], hgrn_fb[1], p, xs, modt,
      jnp.tile(gdn_gain, GROUP_HEADS).reshape(1, gw), jnp.tile(hgrn_gain, GROUP_HEADS).reshape(1, gw),
      g_post.reshape(1, d), g_pre.reshape(1, d), hmb, w_bf16)


def _ffn_kernel(h_ref, hp_ref, hn_ref, x_ref, mod_ref, gain_ref, wup_ref, cw_ref, cb_ref, wdn_ref, o_ref,
                *, ctx_tiles, n_tiles, d_ff, fc):
    t = pl.program_id(1)
    zero_prev = t == 0
    zero_next = t == n_tiles - 1
    if ctx_tiles:
        zero_prev = zero_prev | (t == ctx_tiles)
        zero_next = zero_next | (t == ctx_tiles - 1)
    h = h_ref[0]
    halo = jnp.concatenate([hp_ref[0], hn_ref[0]], axis=0)
    row = lax.broadcasted_iota(jnp.int32, (TM, fc), 0)
    acc = jnp.zeros((TM, x_ref.shape[2]), F32)
    for c0 in range(0, d_ff, fc):
        act = None
        for base in (c0, d_ff + c0):
            w = wup_ref[:, base:base + fc]
            u = _dot(h, w)
            uh = _dot(halo, w)
            prev_row = jnp.where(zero_prev, 0.0, uh[15:16])
            next_row = jnp.where(zero_next, 0.0, uh[16:17])
            u_dn = jnp.where(row == 0, prev_row, pltpu.roll(u, 1, 0))
            u_up = jnp.where(row == TM - 1, next_row, pltpu.roll(u, TM - 1, 0))
            y = (u_dn * cw_ref[0:1, base:base + fc] + u * cw_ref[1:2, base:base + fc]
                 + u_up * cw_ref[2:3, base:base + fc] + cb_ref[0:1, base:base + fc])
            act = _silu(y) if act is None else act * y
        acc = acc + _dot(act.astype(BF16), wdn_ref[c0:c0 + fc, :])
    mod = mod_ref[0, 0]
    o_ref[0] = x_ref[0] + mod[5:6] * _rms(acc, gain_ref[...])


def _ffn(h2, x_mid, modt, gain, wup_bf16, conv_w, conv_b, wdn_bf16, ctx_tiles):
    b, s, d = x_mid.shape
    d_ff = wdn_bf16.shape[0]
    fc = d_ff // 2
    assert fc % 128 == 0
    nt = s // TM
    hb = TM // 16
    mod_off = 0 if ctx_tiles else 1
    single = dict(pipeline_mode=pl.Buffered(1))
    return pl.pallas_call(
        functools.partial(_ffn_kernel, ctx_tiles=ctx_tiles, n_tiles=nt, d_ff=d_ff, fc=fc),
        grid=(b, nt),
        in_specs=[pl.BlockSpec((1, TM, d), lambda i, t: (i, t, 0)),
                  pl.BlockSpec((1, 16, d), lambda i, t: (i, jnp.maximum(t * hb - 1, 0), 0)),
                  pl.BlockSpec((1, 16, d), lambda i, t: (i, jnp.minimum((t + 1) * hb, nt * hb - 1), 0)),
                  pl.BlockSpec((1, TM, d), lambda i, t: (i, t, 0)),
                  pl.BlockSpec((1, 1, N_MOD, d), lambda i, t: (i, jnp.minimum(t + mod_off, 1), 0, 0)),
                  pl.BlockSpec((1, d), lambda i, t: (0, 0)),
                  pl.BlockSpec((d, 2 * d_ff), lambda i, t: (0, 0), **single),
                  pl.BlockSpec((3, 2 * d_ff), lambda i, t: (0, 0)),
                  pl.BlockSpec((1, 2 * d_ff), lambda i, t: (0, 0)),
                  pl.BlockSpec((d_ff, d), lambda i, t: (0, 0), **single)],
        out_specs=pl.BlockSpec((1, TM, d), lambda i, t: (i, t, 0)),
        out_shape=jax.ShapeDtypeStruct((b, s, d), F32),
        compiler_params=_cparams(2),
        name="conv_ffn",
    )(h2, h2, h2, x_mid, modt, gain.reshape(1, d), wup_bf16, conv_w, conv_b.reshape(1, -1), wdn_bf16)


def _rope_tables(n_ctx, length):
    rows = length // GRID_W
    row = jnp.repeat(jnp.arange(rows), GRID_W).astype(F32)
    col = jnp.tile(jnp.arange(GRID_W), rows).astype(F32)
    n_freq = HEAD_DIM // 4
    inv_freq = 1.0 / (ROPE_THETA ** (jnp.arange(n_freq, dtype=F32) / n_freq))
    ang = jnp.concatenate([row[:, None] * inv_freq, col[:, None] * inv_freq], axis=-1)
    cos, sin = jnp.cos(ang), jnp.sin(ang)
    cos_full = jnp.repeat(cos, 2, axis=-1)
    sin_signed = jnp.stack([-sin, sin], axis=-1).reshape(length, HEAD_DIM)
    cos_full = jnp.concatenate([jnp.ones((n_ctx, HEAD_DIM), F32), cos_full], axis=0)
    sin_signed = jnp.concatenate([jnp.zeros((n_ctx, HEAD_DIM), F32), sin_signed], axis=0)
    return jnp.tile(cos_full, (1, GROUP_HEADS)), jnp.tile(sin_signed, (1, GROUP_HEADS))


def _permute_w_in(w):
    pad = jnp.zeros(w.shape[:-1] + (P_WIDTH - w.shape[-1],), w.dtype)
    return jnp.concatenate([w[..., 0:1024], w[..., 1040:], w[..., 1024:1040], pad], axis=-1)


def kernel(x, c, ctx, c_ctx, ada_w, ada_b, norm_pre_mix, norm_post_mix, norm_pre_ffn, norm_post_ffn, w_in, w_out, gdn_conv_w, gdn_a_log, gdn_dt_bias, gdn_norm, attn_q_norm, attn_k_norm, swa_sink, hgrn_lb_raw, hgrn_norm, ffn_w_up, ffn_conv_w, ffn_conv_b, ffn_w_down):
    bsz, length, d = x.shape
    n_ctx = ctx.shape[1]
    depth = ada_w.shape[0]
    assert n_ctx == TM and length % TM == 0 and w_in.shape[-1] == 3344
    ctx_tiles = n_ctx // TM

    xs = jnp.concatenate([ctx, x], axis=1)
    rows = -(-(bsz + 1) // 8) * 8
    cstack = jnp.concatenate([c, c_ctx[None], jnp.zeros((rows - bsz - 1, d), F32)], axis=0)
    mod_all = _ada_mod(cstack, ada_w, ada_b)
    cos256, sin256 = _rope_tables(n_ctx, length)
    w_in_p = _permute_w_in(w_in).astype(BF16)
    w_out_b = w_out.astype(BF16)
    w_up_b = ffn_w_up.astype(BF16)
    w_dn_b = ffn_w_down.astype(BF16)

    for layer in range(depth):
        with_ctx = layer < depth - 1
        mod_l = mod_all[layer, :bsz].reshape(bsz, N_MOD, d)
        mod_c = jnp.broadcast_to(mod_all[layer, bsz].reshape(1, N_MOD, d), (bsz, N_MOD, d))
        modt = jnp.stack([mod_c, mod_l], axis=1)
        p = _inproj(xs, modt, norm_pre_mix[layer], w_in_p[layer])
        gdn_fb = _gdn(p, gdn_conv_w[layer], gdn_a_log[layer], gdn_dt_bias[layer], n_ctx)
        mix_b = _gattn(p, cos256, sin256, attn_q_norm[layer], attn_k_norm[layer], n_ctx, with_ctx)
        mix_c = _wattn(p, cos256, sin256, swa_sink[layer], n_ctx, with_ctx)
        hgrn_fb = _hgrn(p, hgrn_lb_raw, n_ctx, layer)
        t0 = 0 if with_ctx else ctx_tiles
        x_mid, h2 = _outproj(gdn_fb, mix_b, mix_c, hgrn_fb, p, xs, modt, gdn_norm[layer], hgrn_norm[layer],
                             norm_post_mix[layer], norm_pre_ffn[layer], w_out_b[layer], t0)
        xs = _ffn(h2, x_mid, modt, norm_post_ffn[layer], w_up_b[layer], ffn_conv_w[layer],
                  ffn_conv_b[layer], w_dn_b[layer], ctx_tiles if with_ctx else 0)
    return xs
```

```python
import functools

import numpy as np
import jax
import jax.numpy as jnp
from jax import lax
from jax.experimental import pallas as pl
from jax.experimental.pallas import tpu as pltpu

F32 = jnp.float32
BF16 = jnp.bfloat16

HEAD_DIM = 64
GROUP_HEADS = 4
GROUP_WIDTH = GROUP_HEADS * HEAD_DIM
KV_HEADS = 2
KV_WIDTH = KV_HEADS * HEAD_DIM
KV_GROUPS = GROUP_HEADS // KV_HEADS
GRID_W = 64
GDN_CHUNK = 64
HGRN_CHUNK = 16
Q_BLOCK = 128
WINDOW = 128
ROPE_THETA = 10000.0
N_MOD = 6
EPS = 1e-6
NEG = -1e30
TM = 256
GDN_PER_TILE = TM // GDN_CHUNK
HGRN_PER_TILE = TM // HGRN_CHUNK
LOG2E = 1.4426950408889634
ROW_PAD = 16
VMEM_LIMIT = 56 * 1024 * 1024

COL_GDN_QKV = 0
COL_GDN_Z = 768
COL_ATT_Q = 1024
COL_ATT_KV = 1280
COL_SWA_Q = 1536
COL_SWA_KV = 1792
COL_HG_Q = 2048
COL_HG_F = 2304
COL_HG_I = 2816
COL_HG_G = 3072
COL_GDN_BA = 3328
P_WIDTH = 3456


def _cparams(n_axes):
    return pltpu.CompilerParams(dimension_semantics=("arbitrary",) * n_axes,
                                vmem_limit_bytes=VMEM_LIMIT)


def _dot(a, b):
    return jnp.dot(a, b, preferred_element_type=F32)


def _dot_nt(a, b):
    return lax.dot_general(a, b, (((1,), (1,)), ((), ())), preferred_element_type=F32)


def _dot_tn(a, b):
    return lax.dot_general(a, b, (((0,), (0,)), ((), ())), preferred_element_type=F32)


def _split(a, parts):
    out = []
    r = a
    for i in range(parts):
        p = r.astype(BF16)
        out.append(p)
        if i + 1 < parts:
            r = r - p.astype(F32)
    return out


def _mm_x01(a, b01, parts=3):
    m = a.shape[0]
    r = _dot(jnp.concatenate(_split(a, parts), axis=0), b01)
    return sum(r[i * m:(i + 1) * m] for i in range(parts))


def _mm_01x(a01, b, parts=3):
    n = b.shape[1]
    r = _dot(a01, jnp.concatenate(_split(b, parts), axis=1))
    return sum(r[:, i * n:(i + 1) * n] for i in range(parts))


def _mm_hi(a, b):
    ah, al = _split(a, 2)
    bh, bl = _split(b, 2)
    return _dot(ah, bh) + _dot(al, bh) + _dot(ah, bl)


def _silu(x):
    return x * jax.nn.sigmoid(x)


def _softplus(x):
    return jnp.maximum(x, 0.0) + jnp.log1p(jnp.exp(-jnp.abs(x)))


def _head_sumsq(x, ones_bd):
    return _mm_x01(x * x, ones_bd, parts=2)


def _head_rms(x, ones_bd, gain):
    return x * lax.rsqrt(_head_sumsq(x, ones_bd) * (1.0 / HEAD_DIM) + EPS) * gain


def _rope(x, cos, sin_signed):
    w = x.shape[-1]
    lane = lax.broadcasted_iota(jnp.int32, x.shape, x.ndim - 1)
    nxt = pltpu.roll(x, w - 1, x.ndim - 1)
    prv = pltpu.roll(x, 1, x.ndim - 1)
    swapped = jnp.where((lane & 1) == 0, nxt, prv)
    return x * cos + swapped * sin_signed


def _stack_heads(x_bf16, hm_bf16):
    return jnp.concatenate([x_bf16] * GROUP_HEADS, axis=0) * hm_bf16


def _mm_heads(a, b, hm_bf16):
    c = a.shape[0]
    r = _dot(jnp.concatenate(_split(a, 2), axis=0), _stack_heads(b.astype(BF16), hm_bf16))
    return r[0:c] + r[c:2 * c]


def _block_diag_ones(n, blk):
    i = np.arange(n)
    return (i[:, None] // blk == i[None, :] // blk).astype(np.float32)


def _gdn_consts():
    cs = GDN_CHUNK
    hm = _block_diag_ones(GROUP_WIDTH, HEAD_DIM)
    r = np.arange(TM)
    same = r[:, None] // cs == r[None, :] // cs
    tri_rows = np.stack([same & (r[None, :] <= r[:, None]), same & (r[None, :] >= r[:, None])])
    tri_ones = np.concatenate([tri_rows, np.broadcast_to(same, (2, TM, TM))], axis=1).astype(np.float32)
    tri_lanes = np.stack([same & (r[:, None] <= r[None, :]), same & (r[:, None] >= r[None, :])]).astype(np.float32)
    ea = np.zeros((2, 128, GROUP_WIDTH), np.float32)
    eb = np.zeros((2, 128, GROUP_WIDTH), np.float32)
    for d in range(2):
        for h in range(GROUP_HEADS):
            eb[d, d * 4 + h, h * 64:(h + 1) * 64] = 1.0
            ea[d, 8 + d * 4 + h, h * 64:(h + 1) * 64] = 1.0
    i = np.arange(cs)[:, None]
    j = np.arange(GROUP_WIDTH)[None, :] % cs
    incl = np.stack([j <= i, j >= i]).astype(np.float32)
    strict = np.stack([j < i, j > i]).astype(np.float32)
    lvl = np.stack([((i >> (l + 1)) == (j >> (l + 1))) & ((i >> l) != (j >> l)) for l in range(6)]).astype(np.float32)
    eye = (i == j).astype(np.float32)
    last = np.stack([same & (r[:, None] % cs == e) for e in (cs - 1, 0)]).astype(np.float32)
    return [jnp.asarray(hm, BF16), jnp.asarray(tri_ones, BF16), jnp.asarray(tri_lanes, BF16),
            jnp.asarray(ea, BF16), jnp.asarray(eb, BF16), jnp.asarray(incl, F32), jnp.asarray(strict, F32),
            jnp.asarray(lvl, F32), jnp.asarray(eye, F32), jnp.asarray(last, BF16)]


def _hgrn_consts():
    r = np.arange(TM)
    same = r[:, None] // HGRN_CHUNK == r[None, :] // HGRN_CHUNK
    tri = np.stack([same & (r[None, :] <= r[:, None]), same & (r[None, :] >= r[:, None])])
    tri_ones = np.concatenate([tri, np.broadcast_to(same, (2, TM, TM))], axis=1).astype(np.float32)
    chunk_ind = (np.arange(HGRN_PER_TILE)[:, None] == r[None, :] // HGRN_CHUNK).astype(np.float32)
    return [jnp.asarray(_block_diag_ones(TM, HEAD_DIM), BF16), jnp.asarray(tri_ones, BF16),
            jnp.asarray(chunk_ind, BF16)]


def _full_spec(arr):
    nd = arr.ndim
    return pl.BlockSpec(arr.shape, lambda *_: (0,) * nd)


def _flip_tile(t, ctx_tiles, n_tiles):
    return jnp.where(t < ctx_tiles, ctx_tiles - 1 - t, n_tiles - 1 + ctx_tiles - t)


def _ada_kernel(c_ref, w_ref, b_ref, o_ref):
    act = _silu(c_ref[...])
    o_ref[0] = _mm_hi(act, w_ref[0]) + b_ref[0]


def _ada_mod(cstack, ada_w, ada_b):
    depth, d, n = ada_w.shape
    tn = 1536
    rows = cstack.shape[0]
    return pl.pallas_call(
        _ada_kernel,
        grid=(depth, n // tn),
        in_specs=[pl.BlockSpec((rows, d), lambda l, j: (0, 0)),
                  pl.BlockSpec((1, d, tn), lambda l, j: (l, 0, j)),
                  pl.BlockSpec((1, 1, tn), lambda l, j: (l, 0, j))],
        out_specs=pl.BlockSpec((1, rows, tn), lambda l, j: (l, 0, j)),
        out_shape=jax.ShapeDtypeStruct((depth, rows, n), F32),
        compiler_params=_cparams(2),
        name="ada_mod",
    )(cstack, ada_w, ada_b.reshape(depth, 1, n))


def _rms(x, gain):
    return x * lax.rsqrt(jnp.mean(x * x, axis=-1, keepdims=True) + EPS) * gain


def _inproj_kernel(x_ref, mod_ref, gain_ref, w_ref, o_ref):
    mod = mod_ref[0, 0]
    h = _rms(x_ref[0], gain_ref[...]) * (1.0 + mod[1:2]) + mod[0:1]
    o_ref[0] = _dot(h.astype(BF16), w_ref[...])


def _inproj(xs, modt, gain, w_bf16):
    b, s, d = xs.shape
    n = w_bf16.shape[1]
    return pl.pallas_call(
        _inproj_kernel,
        grid=(b, s // TM),
        in_specs=[pl.BlockSpec((1, TM, d), lambda i, t: (i, t, 0)),
                  pl.BlockSpec((1, 1, N_MOD, d), lambda i, t: (i, jnp.minimum(t, 1), 0, 0)),
                  pl.BlockSpec((1, d), lambda i, t: (0, 0)),
                  pl.BlockSpec((d, n), lambda i, t: (0, 0))],
        out_specs=pl.BlockSpec((1, TM, n), lambda i, t: (i, t, 0)),
        out_shape=jax.ShapeDtypeStruct((b, s, n), F32),
        compiler_params=_cparams(2),
        name="inproj",
    )(xs, modt, gain.reshape(1, d), w_bf16)


def _gdn_prep_kernel(qkv_ref, prev_ref, next_ref, ba_ref, arow_ref, cw_ref, alog_ref, dt_ref, alog_row_ref,
                     dt_row_ref, hmb_ref, trio_ref, tril_ref, ea_ref, eb_ref, incl_ref, strict_ref, lvl_ref,
                     eye_ref, last_ref, u_ref, w_ref, at_ref, qd_ref, kd_ref, gl_ref, *, ctx_tiles, n_tiles):
    gw = GROUP_WIDTH
    cs = GDN_CHUNK
    t = pl.program_id(1)
    hmb = hmb_ref[...]

    x = qkv_ref[0]
    zero_prev = (t == 0) | (t == ctx_tiles)
    zero_next = (t == ctx_tiles - 1) | (t == n_tiles - 1)
    prev_row = jnp.where(zero_prev, 0.0, prev_ref[0, 7:8])
    next_row = jnp.where(zero_next, 0.0, next_ref[0, 0:1])
    row = lax.broadcasted_iota(jnp.int32, x.shape, 0)
    x_dn = jnp.where(row == 0, prev_row, pltpu.roll(x, 1, 0))
    x_up = jnp.where(row == TM - 1, next_row, pltpu.roll(x, TM - 1, 0))
    y = _silu(x_dn * cw_ref[0:1] + x * cw_ref[1:2] + x_up * cw_ref[2:3])
    q, k, v = y[:, 0:gw], y[:, gw:2 * gw], y[:, 2 * gw:3 * gw]
    q = q * lax.rsqrt(_head_sumsq(q, hmb) + EPS) * (HEAD_DIM ** -0.5)
    k = k * lax.rsqrt(_head_sumsq(k, hmb) + EPS)

    ba = ba_ref[0]
    beta_c = jax.nn.sigmoid(ba)
    g_c = -jnp.exp(alog_ref[...]) * _softplus(ba + dt_ref[...])

    chains = []
    for d in range(2):
        cums = _mm_01x(trio_ref[d], g_c)
        nat = _mm_x01(cums, ea_ref[d])
        gn_gc, gn_tot = nat[0:TM], nat[TM:]
        gn_beta = _mm_x01(beta_c, eb_ref[d])
        g_row = -jnp.exp(alog_row_ref[d:d + 1]) * _softplus(arow_ref[0, d, 0] + dt_row_ref[d:d + 1])
        gc_row = _mm_x01(g_row, tril_ref[d])
        e_gc = jnp.exp(gn_gc)
        qd_ref[0, d] = (q * e_gc).astype(BF16)
        kd_ref[0, d] = (k * jnp.exp(gn_tot - gn_gc)).astype(BF16)
        vb = v * gn_beta
        kbg = k * gn_beta * e_gc
        gl_ref[0, d, 0] = jnp.exp(_mm_x01(gc_row, last_ref[d]))
        for c in range(GDN_PER_TILE):
            sl = slice(c * cs, (c + 1) * cs)
            decay = jnp.exp(jnp.where(incl_ref[d] > 0.5, gn_gc[sl] - gc_row[c:c + 1], NEG))
            kst = _stack_heads(k[sl].astype(BF16), hmb)
            qk_kk = _dot_nt(jnp.concatenate([q[sl], k[sl]], axis=0).astype(BF16), kst)
            at_ref[0, d, sl, :] = (qk_kk[0:cs] * decay).astype(BF16)
            chains.append((d, sl, strict_ref[d] * gn_beta[sl] * qk_kk[cs:] * decay, vb[sl], kbg[sl]))

    tmats = [eye_ref[...] - ch[2] * lvl_ref[0] for ch in chains]
    for lv in range(1, 6):
        tcs = [_mm_heads(tm, ch[2] * lvl_ref[lv], hmb) for tm, ch in zip(tmats, chains)]
        tmats = [tm - _mm_heads(tc, tm, hmb) for tm, tc in zip(tmats, tcs)]
    for tm, (d, sl, _, vb_c, kbg_c) in zip(tmats, chains):
        u_ref[0, d, sl, :] = _mm_heads(tm, vb_c, hmb)
        w_ref[0, d, sl, :] = _mm_heads(tm, kbg_c, hmb).astype(BF16)


def _gdn_scan_kernel(uf, wf, af, qf, kf, gf, ub, wb, ab, qb, kb, gb, hm_ref, hmb_ref, of_ref, ob_ref, st_s):
    cs = GDN_CHUNK

    @pl.when(pl.program_id(1) == 0)
    def _():
        st_s[...] = jnp.zeros_like(st_s)

    def chunk(refs, state, c):
        u_ref, w_ref, a_ref, q_ref, k_ref, g_ref = refs
        sl = slice(c * cs, (c + 1) * cs)
        r = _dot(jnp.concatenate([w_ref[0, 0, sl, :], q_ref[0, 0, sl, :]], axis=0), state.astype(BF16))
        v_new = (u_ref[0, 0, sl, :] - r[0:cs]).astype(BF16)
        o = r[cs:] + _dot(a_ref[0, 0, sl, :], _stack_heads(v_new, hmb_ref[...]))
        return o, state * g_ref[0, 0, 0, c:c + 1, :] + hm_ref[...] * _dot_tn(k_ref[0, 0, sl, :], v_new)

    st_f, st_b = st_s[0], st_s[1]
    for c in range(GDN_PER_TILE):
        cb = GDN_PER_TILE - 1 - c
        o_f, st_f = chunk((uf, wf, af, qf, kf, gf), st_f, c)
        o_b, st_b = chunk((ub, wb, ab, qb, kb, gb), st_b, cb)
        of_ref[0, c * cs:(c + 1) * cs, :] = o_f
        ob_ref[0, cb * cs:(cb + 1) * cs, :] = o_b
    st_s[0] = st_f
    st_s[1] = st_b


def _gdn(p, conv_w, a_log, dt_bias, n_ctx):
    b, s, _ = p.shape
    gw = GROUP_WIDTH
    nt = s // TM
    ctx_tiles = n_ctx // TM
    consts = _gdn_consts()
    alog_c = jnp.zeros((1, 128), F32).at[0, 8:16].set(a_log.reshape(-1))
    dt_c = jnp.zeros((1, 128), F32).at[0, 8:16].set(dt_bias.reshape(-1))
    alog_row = jnp.repeat(a_log, GDN_CHUNK, axis=1)
    dt_row = jnp.repeat(dt_bias, GDN_CHUNK, axis=1)
    a_raw = p[:, :, COL_GDN_BA + 8:COL_GDN_BA + 16].reshape(b, nt, GDN_PER_TILE, GDN_CHUNK, 2, GROUP_HEADS)
    arow = a_raw.transpose(0, 4, 1, 2, 5, 3).reshape(b, 2, nt, GDN_PER_TILE, gw)
    arow = jnp.pad(arow, ((0, 0), (0, 0), (0, 0), (0, ROW_PAD - GDN_PER_TILE), (0, 0)))
    hb = TM // 8
    dir_spec = pl.BlockSpec((1, 2, TM, gw), lambda i, t: (i, 0, t, 0))
    u, w, at, qd, kd, gl = pl.pallas_call(
        functools.partial(_gdn_prep_kernel, ctx_tiles=ctx_tiles, n_tiles=nt),
        grid=(b, nt),
        in_specs=[pl.BlockSpec((1, TM, 3 * gw), lambda i, t: (i, t, COL_GDN_QKV // (3 * gw))),
                  pl.BlockSpec((1, 8, 3 * gw), lambda i, t: (i, jnp.maximum(t * hb - 1, 0), 0)),
                  pl.BlockSpec((1, 8, 3 * gw), lambda i, t: (i, jnp.minimum((t + 1) * hb, nt * hb - 1), 0)),
                  pl.BlockSpec((1, TM, 128), lambda i, t: (i, t, COL_GDN_BA // 128)),
                  pl.BlockSpec((1, 2, 1, ROW_PAD, gw), lambda i, t: (i, 0, t, 0, 0)),
                  _full_spec(conv_w), _full_spec(alog_c), _full_spec(dt_c), _full_spec(alog_row),
                  _full_spec(dt_row)] + [_full_spec(c) for c in consts],
        out_specs=[dir_spec] * 5 + [pl.BlockSpec((1, 2, 1, ROW_PAD, gw), lambda i, t: (i, 0, t, 0, 0))],
        out_shape=[jax.ShapeDtypeStruct((b, 2, s, gw), F32)] + [jax.ShapeDtypeStruct((b, 2, s, gw), BF16)] * 4
                  + [jax.ShapeDtypeStruct((b, 2, nt, ROW_PAD, gw), F32)],
        compiler_params=_cparams(2),
        name="gdn_prep",
    )(p, p, p, p, arow, conv_w, alog_c, dt_c, alog_row, dt_row, *consts)

    def fwd_spec():
        return pl.BlockSpec((1, 1, TM, gw), lambda i, t: (i, 0, t, 0))

    def bwd_spec():
        return pl.BlockSpec((1, 1, TM, gw), lambda i, t: (i, 1, _flip_tile(t, ctx_tiles, nt), 0))

    gf_spec = pl.BlockSpec((1, 1, 1, ROW_PAD, gw), lambda i, t: (i, 0, t, 0, 0))
    gb_spec = pl.BlockSpec((1, 1, 1, ROW_PAD, gw), lambda i, t: (i, 1, _flip_tile(t, ctx_tiles, nt), 0, 0))
    hm = jnp.asarray(_block_diag_ones(gw, HEAD_DIM), F32)
    hmb = hm.astype(BF16)
    return pl.pallas_call(
        _gdn_scan_kernel,
        grid=(b, nt),
        in_specs=[fwd_spec()] * 5 + [gf_spec] + [bwd_spec()] * 5 + [gb_spec] + [_full_spec(hm), _full_spec(hmb)],
        out_specs=[pl.BlockSpec((1, TM, gw), lambda i, t: (i, t, 0)),
                   pl.BlockSpec((1, TM, gw), lambda i, t: (i, _flip_tile(t, ctx_tiles, nt), 0))],
        out_shape=[jax.ShapeDtypeStruct((b, s, gw), F32)] * 2,
        scratch_shapes=[pltpu.VMEM((2, gw, gw), F32)],
        compiler_params=_cparams(2),
        name="gdn_scan",
    )(u, w, at, qd, kd, gl, u, w, at, qd, kd, gl, hm, hmb)


def _hgrn_prep_kernel(q_ref, f0_ref, f1_ref, i_ref, lbraw_ref, hmb_ref, trio_ref, cind_ref,
                      oin_ref, qd_ref, kd_ref, fl_ref, *, layer):
    depth = lbraw_ref.shape[0]
    raw = [lbraw_ref[l] for l in range(depth)]
    mx = functools.reduce(jnp.maximum, raw)
    ex = [jnp.exp(r - mx) for r in raw]
    lb_all = sum(ex[1:layer + 1], jnp.zeros_like(mx)) / sum(ex)

    q = _silu(q_ref[0])
    v = i_ref[0]
    pos = lax.broadcasted_iota(jnp.int32, (TM, GROUP_WIDTH), 0) % HGRN_CHUNK
    o_acc = jnp.zeros((TM, GROUP_WIDTH), F32)
    for d, f_ref in enumerate((f0_ref, f1_ref)):
        lb = lb_all[d:d + 1]
        z = f_ref[0]
        f = lb + (1.0 - lb) * jax.nn.sigmoid(z)
        logf = jnp.log(f)
        k = (1.0 - lb) * jax.nn.sigmoid(-z)
        cums = _mm_01x(trio_ref[d], logf)
        bc, tot = cums[0:TM], cums[TM:]
        qd_ref[0, d] = (q * jnp.exp(bc)).astype(BF16)
        kd_ref[0, d] = (k * jnp.exp(tot - bc)).astype(BF16)
        fl_ref[0, d, 0] = jnp.exp(_mm_01x(cind_ref[...], logf))
        for delta in range(HGRN_CHUNK):
            if d == 0:
                sh = delta
                ok = pos >= delta
            else:
                sh = (TM - delta) % TM
                ok = pos + delta <= HGRN_CHUNK - 1
            if delta == 0:
                k_sh, bc_sh, v_sh = k, bc, v
            else:
                k_sh = pltpu.roll(k, sh, 0)
                bc_sh = pltpu.roll(bc, sh, 0)
                v_sh = pltpu.roll(v, sh, 0)
            e = q * k_sh * jnp.exp(jnp.where(ok, bc - bc_sh, NEG))
            o_acc = o_acc + _dot(e.astype(BF16), hmb_ref[...]) * v_sh
    oin_ref[0] = o_acc


def _hgrn_scan_kernel(qf, kf, ff, vf, oin, qb, kb, fb, vb, hm_ref, of_ref, ob_ref, st_s):
    cs = HGRN_CHUNK

    @pl.when(pl.program_id(1) == 0)
    def _():
        st_s[...] = jnp.zeros_like(st_s)

    def chunk(q_ref, k_ref, f_ref, v_ref, st, c):
        sl = slice(c * cs, (c + 1) * cs)
        o = _dot_nt(q_ref[0, 0, sl, :], st.astype(BF16))
        upd = _dot_tn(v_ref[0, sl, :].astype(BF16), k_ref[0, 0, sl, :])
        return o, st * f_ref[0, 0, 0, c:c + 1, :] + hm_ref[...] * upd

    st_f, st_b = st_s[0], st_s[1]
    for c in range(HGRN_PER_TILE):
        cb = HGRN_PER_TILE - 1 - c
        o_f, st_f = chunk(qf, kf, ff, vf, st_f, c)
        o_b, st_b = chunk(qb, kb, fb, vb, st_b, cb)
        of_ref[0, c * cs:(c + 1) * cs, :] = o_f + oin[0, c * cs:(c + 1) * cs, :]
        ob_ref[0, cb * cs:(cb + 1) * cs, :] = o_b
    st_s[0] = st_f
    st_s[1] = st_b


def _hgrn(p, lb_raw, n_ctx, layer):
    b, s, _ = p.shape
    gw = GROUP_WIDTH
    nt = s // TM
    ctx_tiles = n_ctx // TM
    consts = _hgrn_consts()

    def col(c):
        return pl.BlockSpec((1, TM, gw), lambda i, t: (i, t, c // gw))

    dir_spec = pl.BlockSpec((1, 2, TM, gw), lambda i, t: (i, 0, t, 0))
    oin, qd, kd, fl = pl.pallas_call(
        functools.partial(_hgrn_prep_kernel, layer=layer),
        grid=(b, nt),
        in_specs=[col(COL_HG_Q), col(COL_HG_F), col(COL_HG_F + gw), col(COL_HG_I), _full_spec(lb_raw)]
                 + [_full_spec(c) for c in consts],
        out_specs=[pl.BlockSpec((1, TM, gw), lambda i, t: (i, t, 0)), dir_spec, dir_spec,
                   pl.BlockSpec((1, 2, 1, HGRN_PER_TILE, gw), lambda i, t: (i, 0, t, 0, 0))],
        out_shape=[jax.ShapeDtypeStruct((b, s, gw), F32), jax.ShapeDtypeStruct((b, 2, s, gw), BF16),
                   jax.ShapeDtypeStruct((b, 2, s, gw), BF16),
                   jax.ShapeDtypeStruct((b, 2, nt, HGRN_PER_TILE, gw), F32)],
        compiler_params=_cparams(2),
        name="hgrn_prep",
    )(p, p, p, p, lb_raw, *consts)

    def flip(t):
        return _flip_tile(t, ctx_tiles, nt)

    hm = jnp.asarray(_block_diag_ones(gw, HEAD_DIM), F32)
    return pl.pallas_call(
        _hgrn_scan_kernel,
        grid=(b, nt),
        in_specs=[pl.BlockSpec((1, 1, TM, gw), lambda i, t: (i, 0, t, 0)),
                  pl.BlockSpec((1, 1, TM, gw), lambda i, t: (i, 0, t, 0)),
                  pl.BlockSpec((1, 1, 1, HGRN_PER_TILE, gw), lambda i, t: (i, 0, t, 0, 0)),
                  pl.BlockSpec((1, TM, gw), lambda i, t: (i, t, COL_HG_I // gw)),
                  pl.BlockSpec((1, TM, gw), lambda i, t: (i, t, 0)),
                  pl.BlockSpec((1, 1, TM, gw), lambda i, t: (i, 1, flip(t), 0)),
                  pl.BlockSpec((1, 1, TM, gw), lambda i, t: (i, 1, flip(t), 0)),
                  pl.BlockSpec((1, 1, 1, HGRN_PER_TILE, gw), lambda i, t: (i, 1, flip(t), 0, 0)),
                  pl.BlockSpec((1, TM, gw), lambda i, t: (i, flip(t), COL_HG_I // gw)),
                  _full_spec(hm)],
        out_specs=[pl.BlockSpec((1, TM, gw), lambda i, t: (i, t, 0)),
                   pl.BlockSpec((1, TM, gw), lambda i, t: (i, flip(t), 0))],
        out_shape=[jax.ShapeDtypeStruct((b, s, gw), F32)] * 2,
        scratch_shapes=[pltpu.VMEM((2, gw, gw), F32)],
        compiler_params=_cparams(2),
        name="hgrn_scan",
    )(qd, kd, fl, p, oin, qd, kd, fl, p, hm)


def _values_with_ones(v):
    ones = jnp.ones((v.shape[0], HEAD_DIM), BF16)
    vb = v.astype(BF16)
    parts = []
    for g in range(KV_HEADS):
        parts += [vb[:, g * HEAD_DIM:(g + 1) * HEAD_DIM], ones]
    return jnp.concatenate(parts, axis=-1)


def _group_queries(q, g):
    return jnp.concatenate([q[:, (g * KV_GROUPS + i) * HEAD_DIM:(g * KV_GROUPS + i + 1) * HEAD_DIM]
                            for i in range(KV_GROUPS)], axis=0)


def _gattn_kernel(q_ref, kv_ref, cq_ref, sq_ref, ck_ref, sk_ref, qg_ref, kg_ref, hm4_ref, hm2_ref,
                  o_ref, k_s, v_s, *, n_ctx, t0):
    s = kv_ref.shape[1]
    t = pl.program_id(1)

    @pl.when(t == 0)
    def _():
        def prep(j, carry):
            r0 = pl.multiple_of(j * TM, TM)
            kv = kv_ref[0, pl.ds(r0, TM), :]
            kn = _head_rms(kv[:, 0:KV_WIDTH], hm2_ref[...], kg_ref[...])
            kn = _rope(kn, ck_ref[pl.ds(r0, TM), :], sk_ref[pl.ds(r0, TM), :])
            k_s[pl.ds(r0, TM), :] = kn.astype(BF16)
            v_s[pl.ds(r0, TM), :] = _values_with_ones(kv[:, KV_WIDTH:])
            return carry
        lax.fori_loop(0, s // TM, prep, 0)

    qn = _head_rms(q_ref[0], hm4_ref[...], qg_ref[...])
    qn = (_rope(qn, cq_ref[...], sq_ref[...]) * (HEAD_DIM ** -0.5 * LOG2E)).astype(BF16)

    def attend(nk):
        outs = []
        for h in range(GROUP_HEADS):
            g = h // KV_GROUPS
            sc = _dot_nt(qn[:, h * HEAD_DIM:(h + 1) * HEAD_DIM], k_s[0:nk, g * HEAD_DIM:(g + 1) * HEAD_DIM])
            pr = jnp.exp2(sc - jnp.max(sc, axis=-1, keepdims=True)).astype(BF16)
            ol = _dot(pr, v_s[0:nk, 2 * g * HEAD_DIM:2 * (g + 1) * HEAD_DIM])
            outs.append(ol[:, 0:HEAD_DIM] / ol[:, HEAD_DIM:HEAD_DIM + 1])
        o_ref[0] = jnp.concatenate(outs, axis=-1).astype(o_ref.dtype)

    if t0 == 0:
        @pl.when(t < n_ctx // TM)
        def _():
            attend(n_ctx)

        @pl.when(t >= n_ctx // TM)
        def _():
            attend(s)
    else:
        attend(s)


def _gattn(p, cos256, sin256, q_gain, k_gain, n_ctx, with_ctx):
    b, s, _ = p.shape
    gw = GROUP_WIDTH
    t0 = 0 if with_ctx else n_ctx // TM
    nt = s // TM - t0
    qg = jnp.tile(q_gain, GROUP_HEADS).reshape(1, gw)
    kg = jnp.tile(k_gain, KV_HEADS).reshape(1, KV_WIDTH)
    hm4 = jnp.asarray(_block_diag_ones(gw, HEAD_DIM), BF16)
    hm2 = jnp.asarray(_block_diag_ones(KV_WIDTH, HEAD_DIM), BF16)
    ck, sk = cos256[:, :KV_WIDTH], sin256[:, :KV_WIDTH]
    return pl.pallas_call(
        functools.partial(_gattn_kernel, n_ctx=n_ctx, t0=t0),
        grid=(b, nt),
        in_specs=[pl.BlockSpec((1, TM, gw), lambda i, t: (i, t + t0, COL_ATT_Q // gw)),
                  pl.BlockSpec((1, s, gw), lambda i, t: (i, 0, COL_ATT_KV // gw)),
                  pl.BlockSpec((TM, gw), lambda i, t: (t + t0, 0)),
                  pl.BlockSpec((TM, gw), lambda i, t: (t + t0, 0)),
                  _full_spec(ck), _full_spec(sk), _full_spec(qg), _full_spec(kg),
                  _full_spec(hm4), _full_spec(hm2)],
        out_specs=pl.BlockSpec((1, TM, gw), lambda i, t: (i, t, 0)),
        out_shape=jax.ShapeDtypeStruct((b, nt * TM, gw), BF16),
        scratch_shapes=[pltpu.VMEM((s, KV_WIDTH), BF16), pltpu.VMEM((s, 2 * KV_WIDTH), BF16)],
        compiler_params=_cparams(2),
        name="global_gqa",
    )(p, p, cos256, sin256, ck, sk, qg, kg, hm4, hm2)


def _wattn_kernel(q_ref, kv_ref, cq_ref, sq_ref, ck_ref, sk_ref, sink_ref, o_ref, k_s, v_s, *, n_ctx, t0):
    s = kv_ref.shape[1]
    qb = Q_BLOCK
    band_w = 3 * qb
    t = pl.program_id(1)
    scale = HEAD_DIM ** -0.5

    @pl.when(t == 0)
    def _():
        def prep(j, carry):
            r0 = pl.multiple_of(j * TM, TM)
            kv = kv_ref[0, pl.ds(r0, TM), :]
            kr = _rope(kv[:, 0:KV_WIDTH], ck_ref[pl.ds(r0, TM), :], sk_ref[pl.ds(r0, TM), :])
            k_s[pl.ds(r0, TM), :] = kr.astype(BF16)
            v_s[pl.ds(r0, TM), :] = _values_with_ones(kv[:, KV_WIDTH:])
            return carry
        lax.fori_loop(0, s // TM, prep, 0)

    qr = (_rope(q_ref[0], cq_ref[...], sq_ref[...]) * scale).astype(BF16)
    sink = sink_ref[...]
    grow = lax.broadcasted_iota(jnp.int32, (KV_GROUPS * qb, 1), 0)

    def attend(keys, vals, valid):
        outs = []
        for g in range(KV_HEADS):
            sc = _dot_nt(_group_queries(qr, g), keys[:, g * HEAD_DIM:(g + 1) * HEAD_DIM])
            if valid is not None:
                sc = jnp.where(valid, sc, NEG)
            sk = sink[:, g * KV_GROUPS * HEAD_DIM:g * KV_GROUPS * HEAD_DIM + 1]
            for i in range(1, KV_GROUPS):
                h = g * KV_GROUPS + i
                sk = jnp.where(grow >= i * qb, sink[:, h * HEAD_DIM:h * HEAD_DIM + 1], sk)
            m = jnp.maximum(jnp.max(sc, axis=-1, keepdims=True), sk)
            ol = _dot(jnp.exp(sc - m).astype(BF16), vals[:, 2 * g * HEAD_DIM:2 * (g + 1) * HEAD_DIM])
            o = ol[:, 0:HEAD_DIM] / (ol[:, HEAD_DIM:HEAD_DIM + 1] + jnp.exp(sk - m))
            outs += [o[i * qb:(i + 1) * qb] for i in range(KV_GROUPS)]
        o_ref[0] = jnp.concatenate(outs, axis=-1).astype(o_ref.dtype)

    def ctx_tile():
        attend(k_s[0:n_ctx, :], v_s[0:n_ctx, :], None)

    def latent_tile():
        n = t + t0 - n_ctx // qb
        start = pl.multiple_of(jnp.minimum(n_ctx - qb + qb * n, s - band_w), qb)
        keys = jnp.concatenate([k_s[pl.ds(start, band_w), :], k_s[0:n_ctx, :]], axis=0)
        vals = jnp.concatenate([v_s[pl.ds(start, band_w), :], v_s[0:n_ctx, :]], axis=0)
        shape = (KV_GROUPS * qb, band_w + n_ctx)
        r = lax.broadcasted_iota(jnp.int32, shape, 0) & (qb - 1)
        c = lax.broadcasted_iota(jnp.int32, shape, 1)
        k_pos = start - n_ctx + c
        dist = n * qb + r - k_pos
        valid = (c >= band_w) | ((jnp.abs(dist) <= WINDOW) & (k_pos >= 0))
        attend(keys, vals, valid)

    if t0 == 0:
        pl.when(t < n_ctx // qb)(ctx_tile)
        pl.when(t >= n_ctx // qb)(latent_tile)
    else:
        latent_tile()


def _wattn(p, cos256, sin256, sink, n_ctx, with_ctx):
    b, s, _ = p.shape
    gw = GROUP_WIDTH
    qb = Q_BLOCK
    assert n_ctx >= qb and s - n_ctx >= 3 * qb and n_ctx % qb == 0
    t0 = 0 if with_ctx else n_ctx // qb
    nt = s // qb - t0
    sink_row = jnp.repeat(sink, HEAD_DIM).reshape(1, gw)
    ck, sk = cos256[:, :KV_WIDTH], sin256[:, :KV_WIDTH]
    return pl.pallas_call(
        functools.partial(_wattn_kernel, n_ctx=n_ctx, t0=t0),
        grid=(b, nt),
        in_specs=[pl.BlockSpec((1, qb, gw), lambda i, t: (i, t + t0, COL_SWA_Q // gw)),
                  pl.BlockSpec((1, s, gw), lambda i, t: (i, 0, COL_SWA_KV // gw)),
                  pl.BlockSpec((qb, gw), lambda i, t: (t + t0, 0)),
                  pl.BlockSpec((qb, gw), lambda i, t: (t + t0, 0)),
                  _full_spec(ck), _full_spec(sk), _full_spec(sink_row)],
        out_specs=pl.BlockSpec((1, qb, gw), lambda i, t: (i, t, 0)),
        out_shape=jax.ShapeDtypeStruct((b, nt * qb, gw), BF16),
        scratch_shapes=[pltpu.VMEM((s, KV_WIDTH), BF16), pltpu.VMEM((s, 2 * KV_WIDTH), BF16)],
        compiler_params=_cparams(2),
        name="window_gqa",
    )(p, p, cos256, sin256, ck, sk, sink_row)


def _outproj_kernel(gf_ref, gb_ref, gz_ref, b_ref, c_ref, hf_ref, hb_ref, hg_ref, x_ref, mod_ref,
                    ggain_ref, hgain_ref, gpost_ref, gpre_ref, hmb_ref, w_ref, xo_ref, h_ref):
    gw = GROUP_WIDTH
    hmb = hmb_ref[...]
    mix_a = _head_rms(gf_ref[0] + gb_ref[0], hmb, ggain_ref[...]) * _silu(gz_ref[0])
    mix_d = _head_rms(hf_ref[0] + hb_ref[0], hmb, hgain_ref[...]) * _silu(hg_ref[0])
    o = (_dot(mix_a.astype(BF16), w_ref[0:gw]) + _dot(b_ref[0], w_ref[gw:2 * gw])
         + _dot(c_ref[0], w_ref[2 * gw:3 * gw]) + _dot(mix_d.astype(BF16), w_ref[3 * gw:4 * gw]))
    mod = mod_ref[0, 0]
    x = x_ref[0] + mod[2:3] * _rms(o, gpost_ref[...])
    xo_ref[0] = x
    h_ref[0] = (_rms(x, gpre_ref[...]) * (1.0 + mod[4:5]) + mod[3:4]).astype(h_ref.dtype)


def _outproj(gdn_fb, attn_b, attn_c, hgrn_fb, p, xs, modt, gdn_gain, hgrn_gain, g_post, g_pre, w_bf16, t0):
    b, s, d = xs.shape
    gw = GROUP_WIDTH
    nt = s // TM - t0

    def full(col=0):
        return pl.BlockSpec((1, TM, gw), lambda i, t: (i, t + t0, col // gw))

    local = pl.BlockSpec((1, TM, gw), lambda i, t: (i, t, 0))
    row = pl.BlockSpec((1, d), lambda i, t: (0, 0))
    hrow = pl.BlockSpec((1, gw), lambda i, t: (0, 0))
    hmb = jnp.asarray(_block_diag_ones(gw, HEAD_DIM), BF16)
    return pl.pallas_call(
        _outproj_kernel,
        grid=(b, nt),
        in_specs=[full(), full(), full(COL_GDN_Z), local, local, full(), full(), full(COL_HG_G),
                  pl.BlockSpec((1, TM, d), lambda i, t: (i, t + t0, 0)),
                  pl.BlockSpec((1, 1, N_MOD, d), lambda i, t: (i, jnp.minimum(t + t0, 1), 0, 0)),
                  hrow, hrow, row, row, _full_spec(hmb),
                  pl.BlockSpec((4 * gw, d), lambda i, t: (0, 0))],
        out_specs=[pl.BlockSpec((1, TM, d), lambda i, t: (i, t, 0)),
                   pl.BlockSpec((1, TM, d), lambda i, t: (i, t, 0))],
        out_shape=[jax.ShapeDtypeStruct((b, nt * TM, d), F32),
                   jax.ShapeDtypeStruct((b, nt * TM, d), BF16)],
        compiler_params=_cparams(2),
        name="outproj",
    )(gdn_fb[0], gdn_fb[1], p, attn_b, attn_c, hgrn_fb[0], hgrn_fb[1], p, xs, modt,
      jnp.tile(gdn_gain, GROUP_HEADS).reshape(1, gw), jnp.tile(hgrn_gain, GROUP_HEADS).reshape(1, gw),
      g_post.reshape(1, d), g_pre.reshape(1, d), hmb, w_bf16)


def _ffn_kernel(h_ref, hp_ref, hn_ref, x_ref, mod_ref, gain_ref, wup_ref, cw_ref, cb_ref, wdn_ref, o_ref,
                *, ctx_tiles, n_tiles, d_ff, fc):
    t = pl.program_id(1)
    zero_prev = t == 0
    zero_next = t == n_tiles - 1
    if ctx_tiles:
        zero_prev = zero_prev | (t == ctx_tiles)
        zero_next = zero_next | (t == ctx_tiles - 1)
    hh = jnp.concatenate([h_ref[0], hp_ref[0], hn_ref[0]], axis=0)
    row = lax.broadcasted_iota(jnp.int32, (TM, fc), 0)
    acc = jnp.zeros((TM, x_ref.shape[2]), F32)
    for c0 in range(0, d_ff, fc):
        act = None
        for base in (c0, d_ff + c0):
            uu = _dot(hh, wup_ref[:, base:base + fc])
            u = uu[0:TM]
            prev_row = jnp.where(zero_prev, 0.0, uu[TM + 15:TM + 16])
            next_row = jnp.where(zero_next, 0.0, uu[TM + 16:TM + 17])
            u_dn = jnp.where(row == 0, prev_row, pltpu.roll(u, 1, 0))
            u_up = jnp.where(row == TM - 1, next_row, pltpu.roll(u, TM - 1, 0))
            y = (u_dn * cw_ref[0:1, base:base + fc] + u * cw_ref[1:2, base:base + fc]
                 + u_up * cw_ref[2:3, base:base + fc] + cb_ref[0:1, base:base + fc])
            act = _silu(y) if act is None else act * y
        acc = acc + _dot(act.astype(BF16), wdn_ref[c0:c0 + fc, :])
    mod = mod_ref[0, 0]
    o_ref[0] = x_ref[0] + mod[5:6] * _rms(acc, gain_ref[...])


def _ffn(h2, x_mid, modt, gain, wup_bf16, conv_w, conv_b, wdn_bf16, ctx_tiles):
    b, s, d = x_mid.shape
    d_ff = wdn_bf16.shape[0]
    fc = d_ff // 2
    assert fc % 128 == 0
    nt = s // TM
    hb = TM // 16
    mod_off = 0 if ctx_tiles else 1
    single = dict(pipeline_mode=pl.Buffered(1))
    return pl.pallas_call(
        functools.partial(_ffn_kernel, ctx_tiles=ctx_tiles, n_tiles=nt, d_ff=d_ff, fc=fc),
        grid=(b, nt),
        in_specs=[pl.BlockSpec((1, TM, d), lambda i, t: (i, t, 0)),
                  pl.BlockSpec((1, 16, d), lambda i, t: (i, jnp.maximum(t * hb - 1, 0), 0)),
                  pl.BlockSpec((1, 16, d), lambda i, t: (i, jnp.minimum((t + 1) * hb, nt * hb - 1), 0)),
                  pl.BlockSpec((1, TM, d), lambda i, t: (i, t, 0)),
                  pl.BlockSpec((1, 1, N_MOD, d), lambda i, t: (i, jnp.minimum(t + mod_off, 1), 0, 0)),
                  pl.BlockSpec((1, d), lambda i, t: (0, 0)),
                  pl.BlockSpec((d, 2 * d_ff), lambda i, t: (0, 0), **single),
                  pl.BlockSpec((3, 2 * d_ff), lambda i, t: (0, 0)),
                  pl.BlockSpec((1, 2 * d_ff), lambda i, t: (0, 0)),
                  pl.BlockSpec((d_ff, d), lambda i, t: (0, 0), **single)],
        out_specs=pl.BlockSpec((1, TM, d), lambda i, t: (i, t, 0)),
        out_shape=jax.ShapeDtypeStruct((b, s, d), F32),
        compiler_params=_cparams(2),
        name="conv_ffn",
    )(h2, h2, h2, x_mid, modt, gain.reshape(1, d), wup_bf16, conv_w, conv_b.reshape(1, -1), wdn_bf16)


def _rope_tables(n_ctx, length):
    rows = length // GRID_W
    row = jnp.repeat(jnp.arange(rows), GRID_W).astype(F32)
    col = jnp.tile(jnp.arange(GRID_W), rows).astype(F32)
    n_freq = HEAD_DIM // 4
    inv_freq = 1.0 / (ROPE_THETA ** (jnp.arange(n_freq, dtype=F32) / n_freq))
    ang = jnp.concatenate([row[:, None] * inv_freq, col[:, None] * inv_freq], axis=-1)
    cos, sin = jnp.cos(ang), jnp.sin(ang)
    cos_full = jnp.repeat(cos, 2, axis=-1)
    sin_signed = jnp.stack([-sin, sin], axis=-1).reshape(length, HEAD_DIM)
    cos_full = jnp.concatenate([jnp.ones((n_ctx, HEAD_DIM), F32), cos_full], axis=0)
    sin_signed = jnp.concatenate([jnp.zeros((n_ctx, HEAD_DIM), F32), sin_signed], axis=0)
    return jnp.tile(cos_full, (1, GROUP_HEADS)), jnp.tile(sin_signed, (1, GROUP_HEADS))


def _permute_w_in(w):
    pad = jnp.zeros(w.shape[:-1] + (P_WIDTH - w.shape[-1],), w.dtype)
    return jnp.concatenate([w[..., 0:1024], w[..., 1040:], w[..., 1024:1040], pad], axis=-1)


def kernel(x, c, ctx, c_ctx, ada_w, ada_b, norm_pre_mix, norm_post_mix, norm_pre_ffn, norm_post_ffn, w_in, w_out, gdn_conv_w, gdn_a_log, gdn_dt_bias, gdn_norm, attn_q_norm, attn_k_norm, swa_sink, hgrn_lb_raw, hgrn_norm, ffn_w_up, ffn_conv_w, ffn_conv_b, ffn_w_down):
    bsz, length, d = x.shape
    n_ctx = ctx.shape[1]
    depth = ada_w.shape[0]
    assert n_ctx == TM and length % TM == 0 and w_in.shape[-1] == 3344
    ctx_tiles = n_ctx // TM

    xs = jnp.concatenate([ctx, x], axis=1)
    rows = -(-(bsz + 1) // 8) * 8
    cstack = jnp.concatenate([c, c_ctx[None], jnp.zeros((rows - bsz - 1, d), F32)], axis=0)
    mod_all = _ada_mod(cstack, ada_w, ada_b)
    cos256, sin256 = _rope_tables(n_ctx, length)
    w_in_p = _permute_w_in(w_in).astype(BF16)
    w_out_b = w_out.astype(BF16)
    w_up_b = ffn_w_up.astype(BF16)
    w_dn_b = ffn_w_down.astype(BF16)

    for layer in range(depth):
        with_ctx = layer < depth - 1
        mod_l = mod_all[layer, :bsz].reshape(bsz, N_MOD, d)
        mod_c = jnp.broadcast_to(mod_all[layer, bsz].reshape(1, N_MOD, d), (bsz, N_MOD, d))
        modt = jnp.stack([mod_c, mod_l], axis=1)
        p = _inproj(xs, modt, norm_pre_mix[layer], w_in_p[layer])
        gdn_fb = _gdn(p, gdn_conv_w[layer], gdn_a_log[layer], gdn_dt_bias[layer], n_ctx)
        mix_b = _gattn(p, cos256, sin256, attn_q_norm[layer], attn_k_norm[layer], n_ctx, with_ctx)
        mix_c = _wattn(p, cos256, sin256, swa_sink[layer], n_ctx, with_ctx)
        hgrn_fb = _hgrn(p, hgrn_lb_raw, n_ctx, layer)
        t0 = 0 if with_ctx else ctx_tiles
        x_mid, h2 = _outproj(gdn_fb, mix_b, mix_c, hgrn_fb, p, xs, modt, gdn_norm[layer], hgrn_norm[layer],
                             norm_post_mix[layer], norm_pre_ffn[layer], w_out_b[layer], t0)
        xs = _ffn(h2, x_mid, modt, norm_post_ffn[layer], w_up_b[layer], ffn_conv_w[layer],
                  ffn_conv_b[layer], w_dn_b[layer], ctx_tiles if with_ctx else 0)
    return xs
```

```python
import functools

import numpy as np
import jax
import jax.numpy as jnp
from jax import lax
from jax.experimental import pallas as pl
from jax.experimental.pallas import tpu as pltpu

F32 = jnp.float32
BF16 = jnp.bfloat16

HEAD_DIM = 64
GROUP_HEADS = 4
GROUP_WIDTH = GROUP_HEADS * HEAD_DIM
KV_HEADS = 2
KV_WIDTH = KV_HEADS * HEAD_DIM
KV_GROUPS = GROUP_HEADS // KV_HEADS
GRID_W = 64
GDN_CHUNK = 64
HGRN_CHUNK = 16
Q_BLOCK = 128
WINDOW = 128
ROPE_THETA = 10000.0
N_MOD = 6
EPS = 1e-6
NEG = -1e30
TM = 256
GDN_PER_TILE = TM // GDN_CHUNK
HGRN_PER_TILE = TM // HGRN_CHUNK
LOG2E = 1.4426950408889634
ROW_PAD = 16
VMEM_LIMIT = 56 * 1024 * 1024

COL_GDN_QKV = 0
COL_GDN_Z = 768
COL_ATT_Q = 1024
COL_ATT_KV = 1280
COL_SWA_Q = 1536
COL_SWA_KV = 1792
COL_HG_Q = 2048
COL_HG_F = 2304
COL_HG_I = 2816
COL_HG_G = 3072
COL_GDN_BA = 3328
P_WIDTH = 3456


def _cparams(n_axes):
    return pltpu.CompilerParams(dimension_semantics=("arbitrary",) * n_axes,
                                vmem_limit_bytes=VMEM_LIMIT)


def _dot(a, b):
    return jnp.dot(a, b, preferred_element_type=F32)


def _dot_nt(a, b):
    return lax.dot_general(a, b, (((1,), (1,)), ((), ())), preferred_element_type=F32)


def _dot_tn(a, b):
    return lax.dot_general(a, b, (((0,), (0,)), ((), ())), preferred_element_type=F32)


def _split(a, parts):
    out = []
    r = a
    for i in range(parts):
        p = r.astype(BF16)
        out.append(p)
        if i + 1 < parts:
            r = r - p.astype(F32)
    return out


def _mm_x01(a, b01, parts=3):
    m = a.shape[0]
    r = _dot(jnp.concatenate(_split(a, parts), axis=0), b01)
    return sum(r[i * m:(i + 1) * m] for i in range(parts))


def _mm_01x(a01, b, parts=3):
    n = b.shape[1]
    r = _dot(a01, jnp.concatenate(_split(b, parts), axis=1))
    return sum(r[:, i * n:(i + 1) * n] for i in range(parts))


def _mm_hi(a, b):
    ah, al = _split(a, 2)
    bh, bl = _split(b, 2)
    return _dot(ah, bh) + _dot(al, bh) + _dot(ah, bl)


def _silu(x):
    return x * jax.nn.sigmoid(x)


def _softplus(x):
    return jnp.maximum(x, 0.0) + jnp.log1p(jnp.exp(-jnp.abs(x)))


def _head_sumsq(x, ones_bd):
    return _mm_x01(x * x, ones_bd, parts=2)


def _head_rms(x, ones_bd, gain):
    return x * lax.rsqrt(_head_sumsq(x, ones_bd) * (1.0 / HEAD_DIM) + EPS) * gain


def _rope(x, cos, sin_signed):
    w = x.shape[-1]
    lane = lax.broadcasted_iota(jnp.int32, x.shape, x.ndim - 1)
    nxt = pltpu.roll(x, w - 1, x.ndim - 1)
    prv = pltpu.roll(x, 1, x.ndim - 1)
    swapped = jnp.where((lane & 1) == 0, nxt, prv)
    return x * cos + swapped * sin_signed


def _stack_heads(x_bf16, hm_bf16):
    return jnp.concatenate([x_bf16] * GROUP_HEADS, axis=0) * hm_bf16


def _mm_heads(a, b, hm_bf16):
    c = a.shape[0]
    r = _dot(jnp.concatenate(_split(a, 2), axis=0), _stack_heads(b.astype(BF16), hm_bf16))
    return r[0:c] + r[c:2 * c]


def _block_diag_ones(n, blk):
    i = np.arange(n)
    return (i[:, None] // blk == i[None, :] // blk).astype(np.float32)


def _gdn_consts():
    cs = GDN_CHUNK
    hm = _block_diag_ones(GROUP_WIDTH, HEAD_DIM)
    r = np.arange(TM)
    same = r[:, None] // cs == r[None, :] // cs
    tri_rows = np.stack([same & (r[None, :] <= r[:, None]), same & (r[None, :] >= r[:, None])])
    tri_ones = np.concatenate([tri_rows, np.broadcast_to(same, (2, TM, TM))], axis=1).astype(np.float32)
    tri_lanes = np.stack([same & (r[:, None] <= r[None, :]), same & (r[:, None] >= r[None, :])]).astype(np.float32)
    ea = np.zeros((2, 128, GROUP_WIDTH), np.float32)
    eb = np.zeros((2, 128, GROUP_WIDTH), np.float32)
    for d in range(2):
        for h in range(GROUP_HEADS):
            eb[d, d * 4 + h, h * 64:(h + 1) * 64] = 1.0
            ea[d, 8 + d * 4 + h, h * 64:(h + 1) * 64] = 1.0
    i = np.arange(cs)[:, None]
    j = np.arange(GROUP_WIDTH)[None, :] % cs
    incl = np.stack([j <= i, j >= i]).astype(np.float32)
    strict = np.stack([j < i, j > i]).astype(np.float32)
    lvl = np.stack([((i >> (l + 1)) == (j >> (l + 1))) & ((i >> l) != (j >> l)) for l in range(6)]).astype(np.float32)
    eye = (i == j).astype(np.float32)
    last = np.stack([same & (r[:, None] % cs == e) for e in (cs - 1, 0)]).astype(np.float32)
    return [jnp.asarray(hm, BF16), jnp.asarray(tri_ones, BF16), jnp.asarray(tri_lanes, BF16),
            jnp.asarray(ea, BF16), jnp.asarray(eb, BF16), jnp.asarray(incl, F32), jnp.asarray(strict, F32),
            jnp.asarray(lvl, F32), jnp.asarray(eye, F32), jnp.asarray(last, BF16)]


def _hgrn_consts():
    r = np.arange(TM)
    same = r[:, None] // HGRN_CHUNK == r[None, :] // HGRN_CHUNK
    tri = np.stack([same & (r[None, :] <= r[:, None]), same & (r[None, :] >= r[:, None])])
    tri_ones = np.concatenate([tri, np.broadcast_to(same, (2, TM, TM))], axis=1).astype(np.float32)
    chunk_ind = (np.arange(HGRN_PER_TILE)[:, None] == r[None, :] // HGRN_CHUNK).astype(np.float32)
    return [jnp.asarray(_block_diag_ones(TM, HEAD_DIM), BF16), jnp.asarray(tri_ones, BF16),
            jnp.asarray(chunk_ind, BF16)]


def _full_spec(arr):
    nd = arr.ndim
    return pl.BlockSpec(arr.shape, lambda *_: (0,) * nd)


def _mod_spec(modt, d, tile_off):
    _, layer, bsz = modt
    return pl.BlockSpec((1, 1, N_MOD, d), lambda i, t: (layer, jnp.where(t + tile_off < 1, bsz, i), 0, 0))


def _flip_tile(t, ctx_tiles, n_tiles):
    return jnp.where(t < ctx_tiles, ctx_tiles - 1 - t, n_tiles - 1 + ctx_tiles - t)


def _ada_kernel(c_ref, w_ref, b_ref, o_ref):
    act = _silu(c_ref[...])
    o_ref[0] = _mm_hi(act, w_ref[0]) + b_ref[0]


def _ada_mod(cstack, ada_w, ada_b):
    depth, d, n = ada_w.shape
    tn = 1536
    rows = cstack.shape[0]
    return pl.pallas_call(
        _ada_kernel,
        grid=(depth, n // tn),
        in_specs=[pl.BlockSpec((rows, d), lambda l, j: (0, 0)),
                  pl.BlockSpec((1, d, tn), lambda l, j: (l, 0, j)),
                  pl.BlockSpec((1, 1, tn), lambda l, j: (l, 0, j))],
        out_specs=pl.BlockSpec((1, rows, tn), lambda l, j: (l, 0, j)),
        out_shape=jax.ShapeDtypeStruct((depth, rows, n), F32),
        compiler_params=_cparams(2),
        name="ada_mod",
    )(cstack, ada_w, ada_b.reshape(depth, 1, n))


def _rms(x, gain):
    return x * lax.rsqrt(jnp.mean(x * x, axis=-1, keepdims=True) + EPS) * gain


def _inproj_kernel(x_ref, mod_ref, gain_ref, w_ref, o_ref):
    mod = mod_ref[0, 0]
    h = _rms(x_ref[0], gain_ref[...]) * (1.0 + mod[1:2]) + mod[0:1]
    o_ref[0] = _dot(h.astype(BF16), w_ref[...])


def _inproj(xs, modt, gain, w_bf16):
    b, s, d = xs.shape
    n = w_bf16.shape[1]
    return pl.pallas_call(
        _inproj_kernel,
        grid=(b, s // TM),
        in_specs=[pl.BlockSpec((1, TM, d), lambda i, t: (i, t, 0)),
                  _mod_spec(modt, d, 0),
                  pl.BlockSpec((1, d), lambda i, t: (0, 0)),
                  pl.BlockSpec((d, n), lambda i, t: (0, 0))],
        out_specs=pl.BlockSpec((1, TM, n), lambda i, t: (i, t, 0)),
        out_shape=jax.ShapeDtypeStruct((b, s, n), F32),
        compiler_params=_cparams(2),
        name="inproj",
    )(xs, modt[0], gain.reshape(1, d), w_bf16)


def _gdn_prep_kernel(qkv_ref, prev_ref, next_ref, ba_ref, arow_ref, cw_ref, alog_ref, dt_ref, alog_row_ref,
                     dt_row_ref, hmb_ref, trio_ref, tril_ref, ea_ref, eb_ref, incl_ref, strict_ref, lvl_ref,
                     eye_ref, last_ref, u_ref, w_ref, at_ref, qd_ref, kd_ref, gl_ref, *, ctx_tiles, n_tiles):
    gw = GROUP_WIDTH
    cs = GDN_CHUNK
    t = pl.program_id(1)
    hmb = hmb_ref[...]

    x = qkv_ref[0]
    zero_prev = (t == 0) | (t == ctx_tiles)
    zero_next = (t == ctx_tiles - 1) | (t == n_tiles - 1)
    prev_row = jnp.where(zero_prev, 0.0, prev_ref[0, 7:8])
    next_row = jnp.where(zero_next, 0.0, next_ref[0, 0:1])
    row = lax.broadcasted_iota(jnp.int32, x.shape, 0)
    x_dn = jnp.where(row == 0, prev_row, pltpu.roll(x, 1, 0))
    x_up = jnp.where(row == TM - 1, next_row, pltpu.roll(x, TM - 1, 0))
    y = _silu(x_dn * cw_ref[0:1] + x * cw_ref[1:2] + x_up * cw_ref[2:3])
    q, k, v = y[:, 0:gw], y[:, gw:2 * gw], y[:, 2 * gw:3 * gw]
    q = q * lax.rsqrt(_head_sumsq(q, hmb) + EPS) * (HEAD_DIM ** -0.5)
    k = k * lax.rsqrt(_head_sumsq(k, hmb) + EPS)

    ba = ba_ref[0]
    beta_c = jax.nn.sigmoid(ba)
    g_c = -jnp.exp(alog_ref[...]) * _softplus(ba + dt_ref[...])

    chains = []
    for d in range(2):
        cums = _mm_01x(trio_ref[d], g_c)
        nat = _mm_x01(cums, ea_ref[d])
        gn_gc, gn_tot = nat[0:TM], nat[TM:]
        gn_beta = _mm_x01(beta_c, eb_ref[d])
        g_row = -jnp.exp(alog_row_ref[d:d + 1]) * _softplus(arow_ref[0, d, 0] + dt_row_ref[d:d + 1])
        gc_row = _mm_x01(g_row, tril_ref[d])
        e_gc = jnp.exp(gn_gc)
        qd_ref[0, d] = (q * e_gc).astype(BF16)
        kd_ref[0, d] = (k * jnp.exp(gn_tot - gn_gc)).astype(BF16)
        vb = v * gn_beta
        kbg = k * gn_beta * e_gc
        gl_ref[0, d, 0] = jnp.exp(_mm_x01(gc_row, last_ref[d]))
        for c in range(GDN_PER_TILE):
            sl = slice(c * cs, (c + 1) * cs)
            decay = jnp.exp(jnp.where(incl_ref[d] > 0.5, gn_gc[sl] - gc_row[c:c + 1], NEG))
            kst = _stack_heads(k[sl].astype(BF16), hmb)
            qk_kk = _dot_nt(jnp.concatenate([q[sl], k[sl]], axis=0).astype(BF16), kst)
            at_ref[0, d, sl, :] = (qk_kk[0:cs] * decay).astype(BF16)
            chains.append((d, sl, strict_ref[d] * gn_beta[sl] * qk_kk[cs:] * decay, vb[sl], kbg[sl]))

    tmats = [eye_ref[...] - ch[2] * lvl_ref[0] for ch in chains]
    for lv in range(1, 6):
        tcs = [_mm_heads(tm, ch[2] * lvl_ref[lv], hmb) for tm, ch in zip(tmats, chains)]
        tmats = [tm - _mm_heads(tc, tm, hmb) for tm, tc in zip(tmats, tcs)]
    for tm, (d, sl, _, vb_c, kbg_c) in zip(tmats, chains):
        u_ref[0, d, sl, :] = _mm_heads(tm, vb_c, hmb)
        w_ref[0, d, sl, :] = _mm_heads(tm, kbg_c, hmb).astype(BF16)


def _gdn_scan_kernel(uf, wf, af, qf, kf, gf, ub, wb, ab, qb, kb, gb, hm_ref, hmb_ref, of_ref, ob_ref, st_s):
    cs = GDN_CHUNK

    @pl.when(pl.program_id(1) == 0)
    def _():
        st_s[...] = jnp.zeros_like(st_s)

    def chunk(refs, state, c):
        u_ref, w_ref, a_ref, q_ref, k_ref, g_ref = refs
        sl = slice(c * cs, (c + 1) * cs)
        r = _dot(jnp.concatenate([w_ref[0, 0, sl, :], q_ref[0, 0, sl, :]], axis=0), state.astype(BF16))
        v_new = (u_ref[0, 0, sl, :] - r[0:cs]).astype(BF16)
        o = r[cs:] + _dot(a_ref[0, 0, sl, :], _stack_heads(v_new, hmb_ref[...]))
        return o, state * g_ref[0, 0, 0, c:c + 1, :] + hm_ref[...] * _dot_tn(k_ref[0, 0, sl, :], v_new)

    st_f, st_b = st_s[0], st_s[1]
    for c in range(GDN_PER_TILE):
        cb = GDN_PER_TILE - 1 - c
        o_f, st_f = chunk((uf, wf, af, qf, kf, gf), st_f, c)
        o_b, st_b = chunk((ub, wb, ab, qb, kb, gb), st_b, cb)
        of_ref[0, c * cs:(c + 1) * cs, :] = o_f
        ob_ref[0, cb * cs:(cb + 1) * cs, :] = o_b
    st_s[0] = st_f
    st_s[1] = st_b


def _gdn(p, conv_w, a_log, dt_bias, n_ctx):
    b, s, _ = p.shape
    gw = GROUP_WIDTH
    nt = s // TM
    ctx_tiles = n_ctx // TM
    consts = _gdn_consts()
    alog_c = jnp.zeros((1, 128), F32).at[0, 8:16].set(a_log.reshape(-1))
    dt_c = jnp.zeros((1, 128), F32).at[0, 8:16].set(dt_bias.reshape(-1))
    alog_row = jnp.repeat(a_log, GDN_CHUNK, axis=1)
    dt_row = jnp.repeat(dt_bias, GDN_CHUNK, axis=1)
    a_raw = p[:, :, COL_GDN_BA + 8:COL_GDN_BA + 16].reshape(b, nt, GDN_PER_TILE, GDN_CHUNK, 2, GROUP_HEADS)
    arow = a_raw.transpose(0, 4, 1, 2, 5, 3).reshape(b, 2, nt, GDN_PER_TILE, gw)
    arow = jnp.pad(arow, ((0, 0), (0, 0), (0, 0), (0, ROW_PAD - GDN_PER_TILE), (0, 0)))
    hb = TM // 8
    dir_spec = pl.BlockSpec((1, 2, TM, gw), lambda i, t: (i, 0, t, 0))
    u, w, at, qd, kd, gl = pl.pallas_call(
        functools.partial(_gdn_prep_kernel, ctx_tiles=ctx_tiles, n_tiles=nt),
        grid=(b, nt),
        in_specs=[pl.BlockSpec((1, TM, 3 * gw), lambda i, t: (i, t, COL_GDN_QKV // (3 * gw))),
                  pl.BlockSpec((1, 8, 3 * gw), lambda i, t: (i, jnp.maximum(t * hb - 1, 0), 0)),
                  pl.BlockSpec((1, 8, 3 * gw), lambda i, t: (i, jnp.minimum((t + 1) * hb, nt * hb - 1), 0)),
                  pl.BlockSpec((1, TM, 128), lambda i, t: (i, t, COL_GDN_BA // 128)),
                  pl.BlockSpec((1, 2, 1, ROW_PAD, gw), lambda i, t: (i, 0, t, 0, 0)),
                  _full_spec(conv_w), _full_spec(alog_c), _full_spec(dt_c), _full_spec(alog_row),
                  _full_spec(dt_row)] + [_full_spec(c) for c in consts],
        out_specs=[dir_spec] * 5 + [pl.BlockSpec((1, 2, 1, ROW_PAD, gw), lambda i, t: (i, 0, t, 0, 0))],
        out_shape=[jax.ShapeDtypeStruct((b, 2, s, gw), F32)] + [jax.ShapeDtypeStruct((b, 2, s, gw), BF16)] * 4
                  + [jax.ShapeDtypeStruct((b, 2, nt, ROW_PAD, gw), F32)],
        compiler_params=_cparams(2),
        name="gdn_prep",
    )(p, p, p, p, arow, conv_w, alog_c, dt_c, alog_row, dt_row, *consts)

    def fwd_spec():
        return pl.BlockSpec((1, 1, TM, gw), lambda i, t: (i, 0, t, 0))

    def bwd_spec():
        return pl.BlockSpec((1, 1, TM, gw), lambda i, t: (i, 1, _flip_tile(t, ctx_tiles, nt), 0))

    gf_spec = pl.BlockSpec((1, 1, 1, ROW_PAD, gw), lambda i, t: (i, 0, t, 0, 0))
    gb_spec = pl.BlockSpec((1, 1, 1, ROW_PAD, gw), lambda i, t: (i, 1, _flip_tile(t, ctx_tiles, nt), 0, 0))
    hm = jnp.asarray(_block_diag_ones(gw, HEAD_DIM), F32)
    hmb = hm.astype(BF16)
    return pl.pallas_call(
        _gdn_scan_kernel,
        grid=(b, nt),
        in_specs=[fwd_spec()] * 5 + [gf_spec] + [bwd_spec()] * 5 + [gb_spec] + [_full_spec(hm), _full_spec(hmb)],
        out_specs=[pl.BlockSpec((1, TM, gw), lambda i, t: (i, t, 0)),
                   pl.BlockSpec((1, TM, gw), lambda i, t: (i, _flip_tile(t, ctx_tiles, nt), 0))],
        out_shape=[jax.ShapeDtypeStruct((b, s, gw), F32)] * 2,
        scratch_shapes=[pltpu.VMEM((2, gw, gw), F32)],
        compiler_params=_cparams(2),
        name="gdn_scan",
    )(u, w, at, qd, kd, gl, u, w, at, qd, kd, gl, hm, hmb)


def _hgrn_prep_kernel(q_ref, f0_ref, f1_ref, i_ref, lbraw_ref, hmb_ref, trio_ref, cind_ref,
                      oin_ref, qd_ref, kd_ref, fl_ref, *, layer):
    depth = lbraw_ref.shape[0]
    raw = [lbraw_ref[l] for l in range(depth)]
    mx = functools.reduce(jnp.maximum, raw)
    ex = [jnp.exp(r - mx) for r in raw]
    lb_all = sum(ex[1:layer + 1], jnp.zeros_like(mx)) / sum(ex)

    q = _silu(q_ref[0])
    v = i_ref[0]
    q_b = q.astype(BF16)
    pos = lax.broadcasted_iota(jnp.int32, (TM, GROUP_WIDTH), 0) % HGRN_CHUNK
    o_acc = jnp.zeros((TM, GROUP_WIDTH), F32)
    for d, f_ref in enumerate((f0_ref, f1_ref)):
        lb = lb_all[d:d + 1]
        z = f_ref[0]
        f = lb + (1.0 - lb) * jax.nn.sigmoid(z)
        logf = jnp.log(f)
        k = (1.0 - lb) * jax.nn.sigmoid(-z)
        cums = _mm_01x(trio_ref[d], logf)
        bc, tot = cums[0:TM], cums[TM:]
        qd_ref[0, d] = (q * jnp.exp(bc)).astype(BF16)
        kd_ref[0, d] = (k * jnp.exp(tot - bc)).astype(BF16)
        fl_ref[0, d, 0] = jnp.exp(_mm_01x(cind_ref[...], logf))
        bc2 = bc * LOG2E
        for delta in range(HGRN_CHUNK):
            if d == 0:
                sh = delta
                ok = pos >= delta
            else:
                sh = (TM - delta) % TM
                ok = pos + delta <= HGRN_CHUNK - 1
            if delta == 0:
                k_sh, bc_sh, v_sh = k, bc2, v
            else:
                k_sh = pltpu.roll(k, sh, 0)
                bc_sh = pltpu.roll(bc2, sh, 0)
                v_sh = pltpu.roll(v, sh, 0)
            e = q_b * k_sh.astype(BF16) * jnp.exp2(jnp.where(ok, bc2 - bc_sh, NEG)).astype(BF16)
            o_acc = o_acc + _dot(e, hmb_ref[...]) * v_sh
    oin_ref[0] = o_acc


def _hgrn_scan_kernel(qf, kf, ff, vf, oin, qb, kb, fb, vb, hm_ref, of_ref, ob_ref, st_s):
    cs = HGRN_CHUNK

    @pl.when(pl.program_id(1) == 0)
    def _():
        st_s[...] = jnp.zeros_like(st_s)

    def chunk(q_ref, k_ref, f_ref, v_ref, st, c):
        sl = slice(c * cs, (c + 1) * cs)
        o = _dot_nt(q_ref[0, 0, sl, :], st.astype(BF16))
        upd = _dot_tn(v_ref[0, sl, :].astype(BF16), k_ref[0, 0, sl, :])
        return o, st * f_ref[0, 0, 0, c:c + 1, :] + hm_ref[...] * upd

    st_f, st_b = st_s[0], st_s[1]
    for c in range(HGRN_PER_TILE):
        cb = HGRN_PER_TILE - 1 - c
        o_f, st_f = chunk(qf, kf, ff, vf, st_f, c)
        o_b, st_b = chunk(qb, kb, fb, vb, st_b, cb)
        of_ref[0, c * cs:(c + 1) * cs, :] = o_f + oin[0, c * cs:(c + 1) * cs, :]
        ob_ref[0, cb * cs:(cb + 1) * cs, :] = o_b
    st_s[0] = st_f
    st_s[1] = st_b


def _hgrn(p, lb_raw, n_ctx, layer):
    b, s, _ = p.shape
    gw = GROUP_WIDTH
    nt = s // TM
    ctx_tiles = n_ctx // TM
    consts = _hgrn_consts()

    def col(c):
        return pl.BlockSpec((1, TM, gw), lambda i, t: (i, t, c // gw))

    dir_spec = pl.BlockSpec((1, 2, TM, gw), lambda i, t: (i, 0, t, 0))
    oin, qd, kd, fl = pl.pallas_call(
        functools.partial(_hgrn_prep_kernel, layer=layer),
        grid=(b, nt),
        in_specs=[col(COL_HG_Q), col(COL_HG_F), col(COL_HG_F + gw), col(COL_HG_I), _full_spec(lb_raw)]
                 + [_full_spec(c) for c in consts],
        out_specs=[pl.BlockSpec((1, TM, gw), lambda i, t: (i, t, 0)), dir_spec, dir_spec,
                   pl.BlockSpec((1, 2, 1, HGRN_PER_TILE, gw), lambda i, t: (i, 0, t, 0, 0))],
        out_shape=[jax.ShapeDtypeStruct((b, s, gw), F32), jax.ShapeDtypeStruct((b, 2, s, gw), BF16),
                   jax.ShapeDtypeStruct((b, 2, s, gw), BF16),
                   jax.ShapeDtypeStruct((b, 2, nt, HGRN_PER_TILE, gw), F32)],
        compiler_params=_cparams(2),
        name="hgrn_prep",
    )(p, p, p, p, lb_raw, *consts)

    def flip(t):
        return _flip_tile(t, ctx_tiles, nt)

    hm = jnp.asarray(_block_diag_ones(gw, HEAD_DIM), F32)
    return pl.pallas_call(
        _hgrn_scan_kernel,
        grid=(b, nt),
        in_specs=[pl.BlockSpec((1, 1, TM, gw), lambda i, t: (i, 0, t, 0)),
                  pl.BlockSpec((1, 1, TM, gw), lambda i, t: (i, 0, t, 0)),
                  pl.BlockSpec((1, 1, 1, HGRN_PER_TILE, gw), lambda i, t: (i, 0, t, 0, 0)),
                  pl.BlockSpec((1, TM, gw), lambda i, t: (i, t, COL_HG_I // gw)),
                  pl.BlockSpec((1, TM, gw), lambda i, t: (i, t, 0)),
                  pl.BlockSpec((1, 1, TM, gw), lambda i, t: (i, 1, flip(t), 0)),
                  pl.BlockSpec((1, 1, TM, gw), lambda i, t: (i, 1, flip(t), 0)),
                  pl.BlockSpec((1, 1, 1, HGRN_PER_TILE, gw), lambda i, t: (i, 1, flip(t), 0, 0)),
                  pl.BlockSpec((1, TM, gw), lambda i, t: (i, flip(t), COL_HG_I // gw)),
                  _full_spec(hm)],
        out_specs=[pl.BlockSpec((1, TM, gw), lambda i, t: (i, t, 0)),
                   pl.BlockSpec((1, TM, gw), lambda i, t: (i, flip(t), 0))],
        out_shape=[jax.ShapeDtypeStruct((b, s, gw), F32)] * 2,
        scratch_shapes=[pltpu.VMEM((2, gw, gw), F32)],
        compiler_params=_cparams(2),
        name="hgrn_scan",
    )(qd, kd, fl, p, oin, qd, kd, fl, p, hm)


def _values_with_ones(v):
    ones = jnp.ones((v.shape[0], HEAD_DIM), BF16)
    vb = v.astype(BF16)
    parts = []
    for g in range(KV_HEADS):
        parts += [vb[:, g * HEAD_DIM:(g + 1) * HEAD_DIM], ones]
    return jnp.concatenate(parts, axis=-1)


def _group_queries(q, g):
    return jnp.concatenate([q[:, (g * KV_GROUPS + i) * HEAD_DIM:(g * KV_GROUPS + i + 1) * HEAD_DIM]
                            for i in range(KV_GROUPS)], axis=0)


def _gattn_kernel(q_ref, kv_ref, cq_ref, sq_ref, ck_ref, sk_ref, qg_ref, kg_ref, hm4_ref, hm2_ref,
                  o_ref, k_s, v_s, *, n_ctx, t0):
    s = kv_ref.shape[1]
    t = pl.program_id(1)

    @pl.when(t == 0)
    def _():
        def prep(j, carry):
            r0 = pl.multiple_of(j * TM, TM)
            kv = kv_ref[0, pl.ds(r0, TM), :]
            kn = _head_rms(kv[:, 0:KV_WIDTH], hm2_ref[...], kg_ref[...])
            kn = _rope(kn, ck_ref[pl.ds(r0, TM), :], sk_ref[pl.ds(r0, TM), :])
            k_s[pl.ds(r0, TM), :] = kn.astype(BF16)
            v_s[pl.ds(r0, TM), :] = _values_with_ones(kv[:, KV_WIDTH:])
            return carry
        lax.fori_loop(0, s // TM, prep, 0)

    qn = _head_rms(q_ref[0], hm4_ref[...], qg_ref[...])
    qn = (_rope(qn, cq_ref[...], sq_ref[...]) * (HEAD_DIM ** -0.5 * LOG2E)).astype(BF16)

    def attend(nk):
        outs = []
        for h in range(GROUP_HEADS):
            g = h // KV_GROUPS
            sc = _dot_nt(qn[:, h * HEAD_DIM:(h + 1) * HEAD_DIM], k_s[0:nk, g * HEAD_DIM:(g + 1) * HEAD_DIM])
            pr = jnp.exp2(sc - jnp.max(sc, axis=-1, keepdims=True)).astype(BF16)
            ol = _dot(pr, v_s[0:nk, 2 * g * HEAD_DIM:2 * (g + 1) * HEAD_DIM])
            outs.append(ol[:, 0:HEAD_DIM] / ol[:, HEAD_DIM:HEAD_DIM + 1])
        o_ref[0] = jnp.concatenate(outs, axis=-1).astype(o_ref.dtype)

    if t0 == 0:
        @pl.when(t < n_ctx // TM)
        def _():
            attend(n_ctx)

        @pl.when(t >= n_ctx // TM)
        def _():
            attend(s)
    else:
        attend(s)


def _gattn(p, cos256, sin256, q_gain, k_gain, n_ctx, with_ctx):
    b, s, _ = p.shape
    gw = GROUP_WIDTH
    t0 = 0 if with_ctx else n_ctx // TM
    nt = s // TM - t0
    qg = jnp.tile(q_gain, GROUP_HEADS).reshape(1, gw)
    kg = jnp.tile(k_gain, KV_HEADS).reshape(1, KV_WIDTH)
    hm4 = jnp.asarray(_block_diag_ones(gw, HEAD_DIM), BF16)
    hm2 = jnp.asarray(_block_diag_ones(KV_WIDTH, HEAD_DIM), BF16)
    ck, sk = cos256[:, :KV_WIDTH], sin256[:, :KV_WIDTH]
    return pl.pallas_call(
        functools.partial(_gattn_kernel, n_ctx=n_ctx, t0=t0),
        grid=(b, nt),
        in_specs=[pl.BlockSpec((1, TM, gw), lambda i, t: (i, t + t0, COL_ATT_Q // gw)),
                  pl.BlockSpec((1, s, gw), lambda i, t: (i, 0, COL_ATT_KV // gw)),
                  pl.BlockSpec((TM, gw), lambda i, t: (t + t0, 0)),
                  pl.BlockSpec((TM, gw), lambda i, t: (t + t0, 0)),
                  _full_spec(ck), _full_spec(sk), _full_spec(qg), _full_spec(kg),
                  _full_spec(hm4), _full_spec(hm2)],
        out_specs=pl.BlockSpec((1, TM, gw), lambda i, t: (i, t, 0)),
        out_shape=jax.ShapeDtypeStruct((b, nt * TM, gw), BF16),
        scratch_shapes=[pltpu.VMEM((s, KV_WIDTH), BF16), pltpu.VMEM((s, 2 * KV_WIDTH), BF16)],
        compiler_params=_cparams(2),
        name="global_gqa",
    )(p, p, cos256, sin256, ck, sk, qg, kg, hm4, hm2)


def _wattn_kernel(q_ref, kv_ref, cq_ref, sq_ref, ck_ref, sk_ref, sink_ref, o_ref, k_s, v_s, *, n_ctx, t0):
    s = kv_ref.shape[1]
    qb = Q_BLOCK
    band_w = 3 * qb
    t = pl.program_id(1)
    scale = HEAD_DIM ** -0.5

    @pl.when(t == 0)
    def _():
        def prep(j, carry):
            r0 = pl.multiple_of(j * TM, TM)
            kv = kv_ref[0, pl.ds(r0, TM), :]
            kr = _rope(kv[:, 0:KV_WIDTH], ck_ref[pl.ds(r0, TM), :], sk_ref[pl.ds(r0, TM), :])
            k_s[pl.ds(r0, TM), :] = kr.astype(BF16)
            v_s[pl.ds(r0, TM), :] = _values_with_ones(kv[:, KV_WIDTH:])
            return carry
        lax.fori_loop(0, s // TM, prep, 0)

    qr = (_rope(q_ref[0], cq_ref[...], sq_ref[...]) * scale).astype(BF16)
    sink = sink_ref[...]
    grow = lax.broadcasted_iota(jnp.int32, (KV_GROUPS * qb, 1), 0)

    def attend(keys, vals, valid):
        outs = []
        for g in range(KV_HEADS):
            sc = _dot_nt(_group_queries(qr, g), keys[:, g * HEAD_DIM:(g + 1) * HEAD_DIM])
            if valid is not None:
                sc = jnp.where(valid, sc, NEG)
            sk = sink[:, g * KV_GROUPS * HEAD_DIM:g * KV_GROUPS * HEAD_DIM + 1]
            for i in range(1, KV_GROUPS):
                h = g * KV_GROUPS + i
                sk = jnp.where(grow >= i * qb, sink[:, h * HEAD_DIM:h * HEAD_DIM + 1], sk)
            m = jnp.maximum(jnp.max(sc, axis=-1, keepdims=True), sk)
            ol = _dot(jnp.exp(sc - m).astype(BF16), vals[:, 2 * g * HEAD_DIM:2 * (g + 1) * HEAD_DIM])
            o = ol[:, 0:HEAD_DIM] / (ol[:, HEAD_DIM:HEAD_DIM + 1] + jnp.exp(sk - m))
            outs += [o[i * qb:(i + 1) * qb] for i in range(KV_GROUPS)]
        o_ref[0] = jnp.concatenate(outs, axis=-1).astype(o_ref.dtype)

    def ctx_tile():
        attend(k_s[0:n_ctx, :], v_s[0:n_ctx, :], None)

    def latent_tile():
        n = t + t0 - n_ctx // qb
        start = pl.multiple_of(jnp.minimum(n_ctx - qb + qb * n, s - band_w), qb)
        keys = jnp.concatenate([k_s[pl.ds(start, band_w), :], k_s[0:n_ctx, :]], axis=0)
        vals = jnp.concatenate([v_s[pl.ds(start, band_w), :], v_s[0:n_ctx, :]], axis=0)
        shape = (KV_GROUPS * qb, band_w + n_ctx)
        r = lax.broadcasted_iota(jnp.int32, shape, 0) & (qb - 1)
        c = lax.broadcasted_iota(jnp.int32, shape, 1)
        k_pos = start - n_ctx + c
        dist = n * qb + r - k_pos
        valid = (c >= band_w) | ((jnp.abs(dist) <= WINDOW) & (k_pos >= 0))
        attend(keys, vals, valid)

    if t0 == 0:
        pl.when(t < n_ctx // qb)(ctx_tile)
        pl.when(t >= n_ctx // qb)(latent_tile)
    else:
        latent_tile()


def _wattn(p, cos256, sin256, sink, n_ctx, with_ctx):
    b, s, _ = p.shape
    gw = GROUP_WIDTH
    qb = Q_BLOCK
    assert n_ctx >= qb and s - n_ctx >= 3 * qb and n_ctx % qb == 0
    t0 = 0 if with_ctx else n_ctx // qb
    nt = s // qb - t0
    sink_row = jnp.repeat(sink, HEAD_DIM).reshape(1, gw)
    ck, sk = cos256[:, :KV_WIDTH], sin256[:, :KV_WIDTH]
    return pl.pallas_call(
        functools.partial(_wattn_kernel, n_ctx=n_ctx, t0=t0),
        grid=(b, nt),
        in_specs=[pl.BlockSpec((1, qb, gw), lambda i, t: (i, t + t0, COL_SWA_Q // gw)),
                  pl.BlockSpec((1, s, gw), lambda i, t: (i, 0, COL_SWA_KV // gw)),
                  pl.BlockSpec((qb, gw), lambda i, t: (t + t0, 0)),
                  pl.BlockSpec((qb, gw), lambda i, t: (t + t0, 0)),
                  _full_spec(ck), _full_spec(sk), _full_spec(sink_row)],
        out_specs=pl.BlockSpec((1, qb, gw), lambda i, t: (i, t, 0)),
        out_shape=jax.ShapeDtypeStruct((b, nt * qb, gw), BF16),
        scratch_shapes=[pltpu.VMEM((s, KV_WIDTH), BF16), pltpu.VMEM((s, 2 * KV_WIDTH), BF16)],
        compiler_params=_cparams(2),
        name="window_gqa",
    )(p, p, cos256, sin256, ck, sk, sink_row)


def _outproj_kernel(gf_ref, gb_ref, gz_ref, b_ref, c_ref, hf_ref, hb_ref, hg_ref, x_ref, mod_ref,
                    ggain_ref, hgain_ref, gpost_ref, gpre_ref, hmb_ref, w_ref, xo_ref, h_ref):
    gw = GROUP_WIDTH
    hmb = hmb_ref[...]
    mix_a = _head_rms(gf_ref[0] + gb_ref[0], hmb, ggain_ref[...]) * _silu(gz_ref[0])
    mix_d = _head_rms(hf_ref[0] + hb_ref[0], hmb, hgain_ref[...]) * _silu(hg_ref[0])
    o = (_dot(mix_a.astype(BF16), w_ref[0:gw]) + _dot(b_ref[0], w_ref[gw:2 * gw])
         + _dot(c_ref[0], w_ref[2 * gw:3 * gw]) + _dot(mix_d.astype(BF16), w_ref[3 * gw:4 * gw]))
    mod = mod_ref[0, 0]
    x = x_ref[0] + mod[2:3] * _rms(o, gpost_ref[...])
    xo_ref[0] = x
    h_ref[0] = (_rms(x, gpre_ref[...]) * (1.0 + mod[4:5]) + mod[3:4]).astype(h_ref.dtype)


def _outproj(gdn_fb, attn_b, attn_c, hgrn_fb, p, xs, modt, gdn_gain, hgrn_gain, g_post, g_pre, w_bf16, t0):
    b, s, d = xs.shape
    gw = GROUP_WIDTH
    nt = s // TM - t0

    def full(col=0):
        return pl.BlockSpec((1, TM, gw), lambda i, t: (i, t + t0, col // gw))

    local = pl.BlockSpec((1, TM, gw), lambda i, t: (i, t, 0))
    row = pl.BlockSpec((1, d), lambda i, t: (0, 0))
    hrow = pl.BlockSpec((1, gw), lambda i, t: (0, 0))
    hmb = jnp.asarray(_block_diag_ones(gw, HEAD_DIM), BF16)
    return pl.pallas_call(
        _outproj_kernel,
        grid=(b, nt),
        in_specs=[full(), full(), full(COL_GDN_Z), local, local, full(), full(), full(COL_HG_G),
                  pl.BlockSpec((1, TM, d), lambda i, t: (i, t + t0, 0)),
                  _mod_spec(modt, d, t0),
                  hrow, hrow, row, row, _full_spec(hmb),
                  pl.BlockSpec((4 * gw, d), lambda i, t: (0, 0))],
        out_specs=[pl.BlockSpec((1, TM, d), lambda i, t: (i, t, 0)),
                   pl.BlockSpec((1, TM, d), lambda i, t: (i, t, 0))],
        out_shape=[jax.ShapeDtypeStruct((b, nt * TM, d), F32),
                   jax.ShapeDtypeStruct((b, nt * TM, d), BF16)],
        compiler_params=_cparams(2),
        name="outproj",
    )(gdn_fb[0], gdn_fb[1], p, attn_b, attn_c, hgrn_fb[0], hgrn_fb[1], p, xs, modt[0],
      jnp.tile(gdn_gain, GROUP_HEADS).reshape(1, gw), jnp.tile(hgrn_gain, GROUP_HEADS).reshape(1, gw),
      g_post.reshape(1, d), g_pre.reshape(1, d), hmb, w_bf16)


def _ffn_kernel(h_ref, hp_ref, hn_ref, x_ref, mod_ref, gain_ref, wup_ref, cw_ref, cb_ref, wdn_ref, o_ref,
                *, ctx_tiles, n_tiles, d_ff, fc):
    t = pl.program_id(1)
    zero_prev = t == 0
    zero_next = t == n_tiles - 1
    if ctx_tiles:
        zero_prev = zero_prev | (t == ctx_tiles)
        zero_next = zero_next | (t == ctx_tiles - 1)
    hh = jnp.concatenate([h_ref[0], hp_ref[0], hn_ref[0]], axis=0)
    row = lax.broadcasted_iota(jnp.int32, (TM, fc), 0)
    acc = jnp.zeros((TM, x_ref.shape[2]), F32)
    for c0 in range(0, d_ff, fc):
        act = None
        for base in (c0, d_ff + c0):
            uu = _dot(hh, wup_ref[:, base:base + fc])
            u = uu[0:TM]
            prev_row = jnp.where(zero_prev, 0.0, uu[TM + 15:TM + 16])
            next_row = jnp.where(zero_next, 0.0, uu[TM + 16:TM + 17])
            u_dn = jnp.where(row == 0, prev_row, pltpu.roll(u, 1, 0))
            u_up = jnp.where(row == TM - 1, next_row, pltpu.roll(u, TM - 1, 0))
            y = (u_dn * cw_ref[0:1, base:base + fc] + u * cw_ref[1:2, base:base + fc]
                 + u_up * cw_ref[2:3, base:base + fc] + cb_ref[0:1, base:base + fc])
            act = _silu(y) if act is None else act * y
        acc = acc + _dot(act.astype(BF16), wdn_ref[c0:c0 + fc, :])
    mod = mod_ref[0, 0]
    o_ref[0] = x_ref[0] + mod[5:6] * _rms(acc, gain_ref[...])


def _ffn(h2, x_mid, modt, gain, wup_bf16, conv_w, conv_b, wdn_bf16, ctx_tiles):
    b, s, d = x_mid.shape
    d_ff = wdn_bf16.shape[0]
    fc = d_ff // 2
    assert fc % 128 == 0
    nt = s // TM
    hb = TM // 16
    mod_off = 0 if ctx_tiles else 1
    single = dict(pipeline_mode=pl.Buffered(1))
    return pl.pallas_call(
        functools.partial(_ffn_kernel, ctx_tiles=ctx_tiles, n_tiles=nt, d_ff=d_ff, fc=fc),
        grid=(b, nt),
        in_specs=[pl.BlockSpec((1, TM, d), lambda i, t: (i, t, 0)),
                  pl.BlockSpec((1, 16, d), lambda i, t: (i, jnp.maximum(t * hb - 1, 0), 0)),
                  pl.BlockSpec((1, 16, d), lambda i, t: (i, jnp.minimum((t + 1) * hb, nt * hb - 1), 0)),
                  pl.BlockSpec((1, TM, d), lambda i, t: (i, t, 0)),
                  _mod_spec(modt, d, mod_off),
                  pl.BlockSpec((1, d), lambda i, t: (0, 0)),
                  pl.BlockSpec((d, 2 * d_ff), lambda i, t: (0, 0), **single),
                  pl.BlockSpec((3, 2 * d_ff), lambda i, t: (0, 0)),
                  pl.BlockSpec((1, 2 * d_ff), lambda i, t: (0, 0)),
                  pl.BlockSpec((d_ff, d), lambda i, t: (0, 0), **single)],
        out_specs=pl.BlockSpec((1, TM, d), lambda i, t: (i, t, 0)),
        out_shape=jax.ShapeDtypeStruct((b, s, d), F32),
        compiler_params=_cparams(2),
        name="conv_ffn",
    )(h2, h2, h2, x_mid, modt[0], gain.reshape(1, d), wup_bf16, conv_w, conv_b.reshape(1, -1), wdn_bf16)


def _rope_tables(n_ctx, length):
    rows = length // GRID_W
    row = np.repeat(np.arange(rows), GRID_W).astype(np.float64)
    col = np.tile(np.arange(GRID_W), rows).astype(np.float64)
    n_freq = HEAD_DIM // 4
    inv_freq = 1.0 / (ROPE_THETA ** (np.arange(n_freq, dtype=np.float64) / n_freq))
    ang = np.concatenate([row[:, None] * inv_freq, col[:, None] * inv_freq], axis=-1)
    cos, sin = np.cos(ang), np.sin(ang)
    cos_full = np.repeat(cos, 2, axis=-1)
    sin_signed = np.stack([-sin, sin], axis=-1).reshape(length, HEAD_DIM)
    cos_full = np.concatenate([np.ones((n_ctx, HEAD_DIM)), cos_full], axis=0)
    sin_signed = np.concatenate([np.zeros((n_ctx, HEAD_DIM)), sin_signed], axis=0)
    return (jnp.asarray(np.tile(cos_full, (1, GROUP_HEADS)), F32),
            jnp.asarray(np.tile(sin_signed, (1, GROUP_HEADS)), F32))


def _permute_w_in(w):
    pad = jnp.zeros(w.shape[:-1] + (P_WIDTH - w.shape[-1],), w.dtype)
    return jnp.concatenate([w[..., 0:1024], w[..., 1040:], w[..., 1024:1040], pad], axis=-1)


def kernel(x, c, ctx, c_ctx, ada_w, ada_b, norm_pre_mix, norm_post_mix, norm_pre_ffn, norm_post_ffn, w_in, w_out, gdn_conv_w, gdn_a_log, gdn_dt_bias, gdn_norm, attn_q_norm, attn_k_norm, swa_sink, hgrn_lb_raw, hgrn_norm, ffn_w_up, ffn_conv_w, ffn_conv_b, ffn_w_down):
    bsz, length, d = x.shape
    n_ctx = ctx.shape[1]
    depth = ada_w.shape[0]
    assert n_ctx == TM and length % TM == 0 and w_in.shape[-1] == 3344
    ctx_tiles = n_ctx // TM

    xs = jnp.concatenate([ctx, x], axis=1)
    rows = -(-(bsz + 1) // 8) * 8
    cstack = jnp.concatenate([c, c_ctx[None], jnp.zeros((rows - bsz - 1, d), F32)], axis=0)
    mod_all = _ada_mod(cstack, ada_w, ada_b).reshape(depth, rows, N_MOD, d)
    cos256, sin256 = _rope_tables(n_ctx, length)
    w_in_p = _permute_w_in(w_in).astype(BF16)
    w_out_b = w_out.astype(BF16)
    w_up_b = ffn_w_up.astype(BF16)
    w_dn_b = ffn_w_down.astype(BF16)

    for layer in range(depth):
        with_ctx = layer < depth - 1
        modt = (mod_all, layer, bsz)
        p = _inproj(xs, modt, norm_pre_mix[layer], w_in_p[layer])
        gdn_fb = _gdn(p, gdn_conv_w[layer], gdn_a_log[layer], gdn_dt_bias[layer], n_ctx)
        mix_b = _gattn(p, cos256, sin256, attn_q_norm[layer], attn_k_norm[layer], n_ctx, with_ctx)
        mix_c = _wattn(p, cos256, sin256, swa_sink[layer], n_ctx, with_ctx)
        hgrn_fb = _hgrn(p, hgrn_lb_raw, n_ctx, layer)
        t0 = 0 if with_ctx else ctx_tiles
        x_mid, h2 = _outproj(gdn_fb, mix_b, mix_c, hgrn_fb, p, xs, modt, gdn_norm[layer], hgrn_norm[layer],
                             norm_post_mix[layer], norm_pre_ffn[layer], w_out_b[layer], t0)
        xs = _ffn(h2, x_mid, modt, norm_post_ffn[layer], w_up_b[layer], ffn_conv_w[layer],
                  ffn_conv_b[layer], w_dn_b[layer], ctx_tiles if with_ctx else 0)
    return xs
```

```python
import functools

import numpy as np
import jax
import jax.numpy as jnp
from jax import lax
from jax.experimental import pallas as pl
from jax.experimental.pallas import tpu as pltpu

F32 = jnp.float32
BF16 = jnp.bfloat16

HEAD_DIM = 64
GROUP_HEADS = 4
GROUP_WIDTH = GROUP_HEADS * HEAD_DIM
KV_HEADS = 2
KV_WIDTH = KV_HEADS * HEAD_DIM
KV_GROUPS = GROUP_HEADS // KV_HEADS
GRID_W = 64
GDN_CHUNK = 64
HGRN_CHUNK = 16
Q_BLOCK = 128
WINDOW = 128
ROPE_THETA = 10000.0
N_MOD = 6
EPS = 1e-6
NEG = -1e30
TM = 256
GDN_PER_TILE = TM // GDN_CHUNK
HGRN_PER_TILE = TM // HGRN_CHUNK
LOG2E = 1.4426950408889634
ROW_PAD = 16
VMEM_LIMIT = 56 * 1024 * 1024

COL_GDN_QKV = 0
COL_GDN_Z = 768
COL_ATT_Q = 1024
COL_ATT_KV = 1280
COL_SWA_Q = 1536
COL_SWA_KV = 1792
COL_HG_Q = 2048
COL_HG_F = 2304
COL_HG_I = 2816
COL_HG_G = 3072
COL_GDN_BA = 3328
P_WIDTH = 3456


def _cparams(n_axes):
    return pltpu.CompilerParams(dimension_semantics=("arbitrary",) * n_axes,
                                vmem_limit_bytes=VMEM_LIMIT)


def _dot(a, b):
    return jnp.dot(a, b, preferred_element_type=F32)


def _dot_nt(a, b):
    return lax.dot_general(a, b, (((1,), (1,)), ((), ())), preferred_element_type=F32)


def _dot_tn(a, b):
    return lax.dot_general(a, b, (((0,), (0,)), ((), ())), preferred_element_type=F32)


def _split(a, parts):
    out = []
    r = a
    for i in range(parts):
        p = r.astype(BF16)
        out.append(p)
        if i + 1 < parts:
            r = r - p.astype(F32)
    return out


def _mm_x01(a, b01, parts=3):
    m = a.shape[0]
    r = _dot(jnp.concatenate(_split(a, parts), axis=0), b01)
    return sum(r[i * m:(i + 1) * m] for i in range(parts))


def _mm_01x(a01, b, parts=3):
    n = b.shape[1]
    r = _dot(a01, jnp.concatenate(_split(b, parts), axis=1))
    return sum(r[:, i * n:(i + 1) * n] for i in range(parts))


def _mm_hi(a, b):
    ah, al = _split(a, 2)
    bh, bl = _split(b, 2)
    return _dot(ah, bh) + _dot(al, bh) + _dot(ah, bl)


def _silu(x):
    return x * jax.nn.sigmoid(x)


def _softplus(x):
    return jnp.maximum(x, 0.0) + jnp.log1p(jnp.exp(-jnp.abs(x)))


def _head_sumsq(x, ones_bd):
    return _mm_x01(x * x, ones_bd, parts=2)


def _head_rms(x, ones_bd, gain):
    return x * lax.rsqrt(_head_sumsq(x, ones_bd) * (1.0 / HEAD_DIM) + EPS) * gain


def _rope(x, cos, sin_signed):
    w = x.shape[-1]
    lane = lax.broadcasted_iota(jnp.int32, x.shape, x.ndim - 1)
    nxt = pltpu.roll(x, w - 1, x.ndim - 1)
    prv = pltpu.roll(x, 1, x.ndim - 1)
    swapped = jnp.where((lane & 1) == 0, nxt, prv)
    return x * cos + swapped * sin_signed


def _stack_heads(x_bf16, hm_bf16):
    return jnp.concatenate([x_bf16] * GROUP_HEADS, axis=0) * hm_bf16


def _mm_heads(a, b, hm_bf16):
    c = a.shape[0]
    r = _dot(jnp.concatenate(_split(a, 2), axis=0), _stack_heads(b.astype(BF16), hm_bf16))
    return r[0:c] + r[c:2 * c]


def _block_diag_ones(n, blk):
    i = np.arange(n)
    return (i[:, None] // blk == i[None, :] // blk).astype(np.float32)


def _gdn_consts():
    cs = GDN_CHUNK
    hm = _block_diag_ones(GROUP_WIDTH, HEAD_DIM)
    r = np.arange(TM)
    same = r[:, None] // cs == r[None, :] // cs
    tri_rows = np.stack([same & (r[None, :] <= r[:, None]), same & (r[None, :] >= r[:, None])])
    tri_ones = np.concatenate([tri_rows, np.broadcast_to(same, (2, TM, TM))], axis=1).astype(np.float32)
    tri_lanes = np.stack([same & (r[:, None] <= r[None, :]), same & (r[:, None] >= r[None, :])]).astype(np.float32)
    ea = np.zeros((2, 128, GROUP_WIDTH), np.float32)
    eb = np.zeros((2, 128, GROUP_WIDTH), np.float32)
    for d in range(2):
        for h in range(GROUP_HEADS):
            eb[d, d * 4 + h, h * 64:(h + 1) * 64] = 1.0
            ea[d, 8 + d * 4 + h, h * 64:(h + 1) * 64] = 1.0
    i = np.arange(cs)[:, None]
    j = np.arange(GROUP_WIDTH)[None, :] % cs
    incl = np.stack([j <= i, j >= i]).astype(np.float32)
    strict = np.stack([j < i, j > i]).astype(np.float32)
    lvl = np.stack([((i >> (l + 1)) == (j >> (l + 1))) & ((i >> l) != (j >> l)) for l in range(6)]).astype(np.float32)
    eye = (i == j).astype(np.float32)
    last = np.stack([same & (r[:, None] % cs == e) for e in (cs - 1, 0)]).astype(np.float32)
    return [jnp.asarray(hm, BF16), jnp.asarray(tri_ones, BF16), jnp.asarray(tri_lanes, BF16),
            jnp.asarray(ea, BF16), jnp.asarray(eb, BF16), jnp.asarray(incl, F32), jnp.asarray(strict, F32),
            jnp.asarray(lvl, F32), jnp.asarray(eye, F32), jnp.asarray(last, BF16)]


def _hgrn_consts():
    r = np.arange(TM)
    same = r[:, None] // HGRN_CHUNK == r[None, :] // HGRN_CHUNK
    tri = np.stack([same & (r[None, :] <= r[:, None]), same & (r[None, :] >= r[:, None])])
    tri_ones = np.concatenate([tri, np.broadcast_to(same, (2, TM, TM))], axis=1).astype(np.float32)
    chunk_ind = (np.arange(HGRN_PER_TILE)[:, None] == r[None, :] // HGRN_CHUNK).astype(np.float32)
    return [jnp.asarray(_block_diag_ones(TM, HEAD_DIM), BF16), jnp.asarray(tri_ones, BF16),
            jnp.asarray(chunk_ind, BF16)]


def _full_spec(arr):
    nd = arr.ndim
    return pl.BlockSpec(arr.shape, lambda *_: (0,) * nd)


def _mod_spec(modt, d, tile_off):
    _, layer, bsz = modt
    return pl.BlockSpec((1, 1, N_MOD, d), lambda i, t: (layer, jnp.where(t + tile_off < 1, bsz, i), 0, 0))


def _flip_tile(t, ctx_tiles, n_tiles):
    return jnp.where(t < ctx_tiles, ctx_tiles - 1 - t, n_tiles - 1 + ctx_tiles - t)


def _ada_kernel(c_ref, w_ref, b_ref, o_ref):
    act = _silu(c_ref[...])
    o_ref[0] = _mm_hi(act, w_ref[0]) + b_ref[0]


def _ada_mod(cstack, ada_w, ada_b):
    depth, d, n = ada_w.shape
    tn = 1536
    rows = cstack.shape[0]
    return pl.pallas_call(
        _ada_kernel,
        grid=(depth, n // tn),
        in_specs=[pl.BlockSpec((rows, d), lambda l, j: (0, 0)),
                  pl.BlockSpec((1, d, tn), lambda l, j: (l, 0, j)),
                  pl.BlockSpec((1, 1, tn), lambda l, j: (l, 0, j))],
        out_specs=pl.BlockSpec((1, rows, tn), lambda l, j: (l, 0, j)),
        out_shape=jax.ShapeDtypeStruct((depth, rows, n), F32),
        compiler_params=_cparams(2),
        name="ada_mod",
    )(cstack, ada_w, ada_b.reshape(depth, 1, n))


def _rms(x, gain):
    return x * lax.rsqrt(jnp.mean(x * x, axis=-1, keepdims=True) + EPS) * gain


def _inproj_kernel(x_ref, mod_ref, gain_ref, w_ref, o_ref):
    mod = mod_ref[0, 0]
    h = _rms(x_ref[0], gain_ref[...]) * (1.0 + mod[1:2]) + mod[0:1]
    o_ref[0] = _dot(h.astype(BF16), w_ref[0])


def _inproj(xs, modt, gain, w_bf16):
    b, s, d = xs.shape
    n = w_bf16.shape[2]
    layer = modt[1]
    return pl.pallas_call(
        _inproj_kernel,
        grid=(b, s // TM),
        in_specs=[pl.BlockSpec((1, TM, d), lambda i, t: (i, t, 0)),
                  _mod_spec(modt, d, 0),
                  pl.BlockSpec((1, d), lambda i, t: (0, 0)),
                  pl.BlockSpec((1, d, n), lambda i, t: (layer, 0, 0))],
        out_specs=pl.BlockSpec((1, TM, n), lambda i, t: (i, t, 0)),
        out_shape=jax.ShapeDtypeStruct((b, s, n), F32),
        compiler_params=_cparams(2),
        name="inproj",
    )(xs, modt[0], gain.reshape(1, d), w_bf16)


def _gdn_prep_kernel(qkv_ref, prev_ref, next_ref, ba_ref, arow_ref, cw_ref, alog_ref, dt_ref, alog_row_ref,
                     dt_row_ref, hmb_ref, trio_ref, tril_ref, ea_ref, eb_ref, incl_ref, strict_ref, lvl_ref,
                     eye_ref, last_ref, u_ref, w_ref, at_ref, qd_ref, kd_ref, gl_ref, *, ctx_tiles, n_tiles):
    gw = GROUP_WIDTH
    cs = GDN_CHUNK
    t = pl.program_id(1)
    hmb = hmb_ref[...]

    x = qkv_ref[0]
    zero_prev = (t == 0) | (t == ctx_tiles)
    zero_next = (t == ctx_tiles - 1) | (t == n_tiles - 1)
    prev_row = jnp.where(zero_prev, 0.0, prev_ref[0, 7:8])
    next_row = jnp.where(zero_next, 0.0, next_ref[0, 0:1])
    row = lax.broadcasted_iota(jnp.int32, x.shape, 0)
    x_dn = jnp.where(row == 0, prev_row, pltpu.roll(x, 1, 0))
    x_up = jnp.where(row == TM - 1, next_row, pltpu.roll(x, TM - 1, 0))
    y = _silu(x_dn * cw_ref[0:1] + x * cw_ref[1:2] + x_up * cw_ref[2:3])
    q, k, v = y[:, 0:gw], y[:, gw:2 * gw], y[:, 2 * gw:3 * gw]
    q = q * lax.rsqrt(_head_sumsq(q, hmb) + EPS) * (HEAD_DIM ** -0.5)
    k = k * lax.rsqrt(_head_sumsq(k, hmb) + EPS)

    ba = ba_ref[0]
    beta_c = jax.nn.sigmoid(ba)
    g_c = -jnp.exp(alog_ref[...]) * _softplus(ba + dt_ref[...])

    chains = []
    for d in range(2):
        cums = _mm_01x(trio_ref[d], g_c)
        nat = _mm_x01(cums, ea_ref[d])
        gn_gc, gn_tot = nat[0:TM], nat[TM:]
        gn_beta = _mm_x01(beta_c, eb_ref[d])
        g_row = -jnp.exp(alog_row_ref[d:d + 1]) * _softplus(arow_ref[0, d, 0] + dt_row_ref[d:d + 1])
        gc_row = _mm_x01(g_row, tril_ref[d])
        e_gc = jnp.exp(gn_gc)
        qd_ref[0, d] = (q * e_gc).astype(BF16)
        kd_ref[0, d] = (k * jnp.exp(gn_tot - gn_gc)).astype(BF16)
        vb = v * gn_beta
        kbg = k * gn_beta * e_gc
        gl_ref[0, d, 0] = jnp.exp(_mm_x01(gc_row, last_ref[d]))
        for c in range(GDN_PER_TILE):
            sl = slice(c * cs, (c + 1) * cs)
            decay = jnp.exp(jnp.where(incl_ref[d] > 0.5, gn_gc[sl] - gc_row[c:c + 1], NEG))
            kst = _stack_heads(k[sl].astype(BF16), hmb)
            qk_kk = _dot_nt(jnp.concatenate([q[sl], k[sl]], axis=0).astype(BF16), kst)
            at_ref[0, d, sl, :] = (qk_kk[0:cs] * decay).astype(BF16)
            chains.append((d, sl, strict_ref[d] * gn_beta[sl] * qk_kk[cs:] * decay, vb[sl], kbg[sl]))

    tmats = [eye_ref[...] - ch[2] * lvl_ref[0] for ch in chains]
    for lv in range(1, 6):
        tcs = [_mm_heads(tm, ch[2] * lvl_ref[lv], hmb) for tm, ch in zip(tmats, chains)]
        tmats = [tm - _mm_heads(tc, tm, hmb) for tm, tc in zip(tmats, tcs)]
    for tm, (d, sl, _, vb_c, kbg_c) in zip(tmats, chains):
        u_ref[0, d, sl, :] = _mm_heads(tm, vb_c, hmb)
        w_ref[0, d, sl, :] = _mm_heads(tm, kbg_c, hmb).astype(BF16)


def _scan_kernel(uf, wf, af, qf, kf, gf, ub, wb, ab, qb, kb, gb,
                 hqf, hkf, hff, hvf, hoin, hqb, hkb, hfb, hvb, hm_ref, hmb_ref,
                 gof_ref, gob_ref, hof_ref, hob_ref, gst_s, hst_s):
    gcs, hcs = GDN_CHUNK, HGRN_CHUNK

    @pl.when(pl.program_id(1) == 0)
    def _():
        gst_s[...] = jnp.zeros_like(gst_s)
        hst_s[...] = jnp.zeros_like(hst_s)

    def gdn_chunk(refs, state, c):
        u_ref, w_ref, a_ref, q_ref, k_ref, g_ref = refs
        sl = slice(c * gcs, (c + 1) * gcs)
        r = _dot(jnp.concatenate([w_ref[0, 0, sl, :], q_ref[0, 0, sl, :]], axis=0), state.astype(BF16))
        v_new = (u_ref[0, 0, sl, :] - r[0:gcs]).astype(BF16)
        o = r[gcs:] + _dot(a_ref[0, 0, sl, :], _stack_heads(v_new, hmb_ref[...]))
        return o, state * g_ref[0, 0, 0, c:c + 1, :] + hm_ref[...] * _dot_tn(k_ref[0, 0, sl, :], v_new)

    def hgrn_chunk(q_ref, k_ref, f_ref, v_ref, st, c):
        sl = slice(c * hcs, (c + 1) * hcs)
        o = _dot_nt(q_ref[0, 0, sl, :], st.astype(BF16))
        upd = _dot_tn(v_ref[0, sl, :].astype(BF16), k_ref[0, 0, sl, :])
        return o, st * f_ref[0, 0, 0, c:c + 1, :] + hm_ref[...] * upd

    g_f, g_b = gst_s[0], gst_s[1]
    h_f, h_b = hst_s[0], hst_s[1]
    ratio = HGRN_PER_TILE // GDN_PER_TILE
    for c in range(GDN_PER_TILE):
        cb = GDN_PER_TILE - 1 - c
        o_f, g_f = gdn_chunk((uf, wf, af, qf, kf, gf), g_f, c)
        o_b, g_b = gdn_chunk((ub, wb, ab, qb, kb, gb), g_b, cb)
        gof_ref[0, c * gcs:(c + 1) * gcs, :] = o_f
        gob_ref[0, cb * gcs:(cb + 1) * gcs, :] = o_b
        for j in range(c * ratio, (c + 1) * ratio):
            jb = HGRN_PER_TILE - 1 - j
            o_f, h_f = hgrn_chunk(hqf, hkf, hff, hvf, h_f, j)
            o_b, h_b = hgrn_chunk(hqb, hkb, hfb, hvb, h_b, jb)
            hof_ref[0, j * hcs:(j + 1) * hcs, :] = o_f + hoin[0, j * hcs:(j + 1) * hcs, :]
            hob_ref[0, jb * hcs:(jb + 1) * hcs, :] = o_b
    gst_s[0] = g_f
    gst_s[1] = g_b
    hst_s[0] = h_f
    hst_s[1] = h_b


def _scans(gdn_parts, hgrn_parts, p, n_ctx):
    u, w, at, qd, kd, gl = gdn_parts
    oin, hqd, hkd, hfl = hgrn_parts
    b, s, gw = oin.shape
    nt = s // TM
    ctx_tiles = n_ctx // TM

    def flip(t):
        return _flip_tile(t, ctx_tiles, nt)

    fwd = pl.BlockSpec((1, 1, TM, gw), lambda i, t: (i, 0, t, 0))
    bwd = pl.BlockSpec((1, 1, TM, gw), lambda i, t: (i, 1, flip(t), 0))

    def rows_spec(n_rows, d):
        return pl.BlockSpec((1, 1, 1, n_rows, gw), lambda i, t: (i, d, flip(t) if d else t, 0, 0))

    v_f = pl.BlockSpec((1, TM, gw), lambda i, t: (i, t, COL_HG_I // gw))
    v_b = pl.BlockSpec((1, TM, gw), lambda i, t: (i, flip(t), COL_HG_I // gw))
    out_f = pl.BlockSpec((1, TM, gw), lambda i, t: (i, t, 0))
    out_b = pl.BlockSpec((1, TM, gw), lambda i, t: (i, flip(t), 0))
    hm = jnp.asarray(_block_diag_ones(gw, HEAD_DIM), F32)
    hmb = hm.astype(BF16)
    gof, gob, hof, hob = pl.pallas_call(
        _scan_kernel,
        grid=(b, nt),
        in_specs=[fwd] * 5 + [rows_spec(ROW_PAD, 0)] + [bwd] * 5 + [rows_spec(ROW_PAD, 1)]
                 + [fwd, fwd, rows_spec(HGRN_PER_TILE, 0), v_f, out_f, bwd, bwd, rows_spec(HGRN_PER_TILE, 1), v_b]
                 + [_full_spec(hm), _full_spec(hmb)],
        out_specs=[out_f, out_b, out_f, out_b],
        out_shape=[jax.ShapeDtypeStruct((b, s, gw), F32)] * 4,
        scratch_shapes=[pltpu.VMEM((2, gw, gw), F32), pltpu.VMEM((2, gw, gw), F32)],
        compiler_params=_cparams(2),
        name="mixer_scans",
    )(u, w, at, qd, kd, gl, u, w, at, qd, kd, gl, hqd, hkd, hfl, p, oin, hqd, hkd, hfl, p, hm, hmb)
    return (gof, gob), (hof, hob)


def _gdn_prep(p, conv_w, a_log, dt_bias, n_ctx):
    b, s, _ = p.shape
    gw = GROUP_WIDTH
    nt = s // TM
    ctx_tiles = n_ctx // TM
    consts = _gdn_consts()
    alog_c = jnp.zeros((1, 128), F32).at[0, 8:16].set(a_log.reshape(-1))
    dt_c = jnp.zeros((1, 128), F32).at[0, 8:16].set(dt_bias.reshape(-1))
    alog_row = jnp.repeat(a_log, GDN_CHUNK, axis=1)
    dt_row = jnp.repeat(dt_bias, GDN_CHUNK, axis=1)
    a_raw = p[:, :, COL_GDN_BA + 8:COL_GDN_BA + 16].reshape(b, nt, GDN_PER_TILE, GDN_CHUNK, 2, GROUP_HEADS)
    arow = a_raw.transpose(0, 4, 1, 2, 5, 3).reshape(b, 2, nt, GDN_PER_TILE, gw)
    arow = jnp.pad(arow, ((0, 0), (0, 0), (0, 0), (0, ROW_PAD - GDN_PER_TILE), (0, 0)))
    hb = TM // 8
    dir_spec = pl.BlockSpec((1, 2, TM, gw), lambda i, t: (i, 0, t, 0))
    u, w, at, qd, kd, gl = pl.pallas_call(
        functools.partial(_gdn_prep_kernel, ctx_tiles=ctx_tiles, n_tiles=nt),
        grid=(b, nt),
        in_specs=[pl.BlockSpec((1, TM, 3 * gw), lambda i, t: (i, t, COL_GDN_QKV // (3 * gw))),
                  pl.BlockSpec((1, 8, 3 * gw), lambda i, t: (i, jnp.maximum(t * hb - 1, 0), 0)),
                  pl.BlockSpec((1, 8, 3 * gw), lambda i, t: (i, jnp.minimum((t + 1) * hb, nt * hb - 1), 0)),
                  pl.BlockSpec((1, TM, 128), lambda i, t: (i, t, COL_GDN_BA // 128)),
                  pl.BlockSpec((1, 2, 1, ROW_PAD, gw), lambda i, t: (i, 0, t, 0, 0)),
                  _full_spec(conv_w), _full_spec(alog_c), _full_spec(dt_c), _full_spec(alog_row),
                  _full_spec(dt_row)] + [_full_spec(c) for c in consts],
        out_specs=[dir_spec] * 5 + [pl.BlockSpec((1, 2, 1, ROW_PAD, gw), lambda i, t: (i, 0, t, 0, 0))],
        out_shape=[jax.ShapeDtypeStruct((b, 2, s, gw), F32)] + [jax.ShapeDtypeStruct((b, 2, s, gw), BF16)] * 4
                  + [jax.ShapeDtypeStruct((b, 2, nt, ROW_PAD, gw), F32)],
        compiler_params=_cparams(2),
        name="gdn_prep",
    )(p, p, p, p, arow, conv_w, alog_c, dt_c, alog_row, dt_row, *consts)

    return u, w, at, qd, kd, gl


def _hgrn_prep_kernel(q_ref, f0_ref, f1_ref, i_ref, lbraw_ref, hmb_ref, trio_ref, cind_ref,
                      oin_ref, qd_ref, kd_ref, fl_ref, *, layer):
    depth = lbraw_ref.shape[0]
    raw = [lbraw_ref[l] for l in range(depth)]
    mx = functools.reduce(jnp.maximum, raw)
    ex = [jnp.exp(r - mx) for r in raw]
    lb_all = sum(ex[1:layer + 1], jnp.zeros_like(mx)) / sum(ex)

    q = _silu(q_ref[0])
    v = i_ref[0]
    q_b = q.astype(BF16)
    pos = lax.broadcasted_iota(jnp.int32, (TM, GROUP_WIDTH), 0) % HGRN_CHUNK
    o_acc = jnp.zeros((TM, GROUP_WIDTH), F32)
    for d, f_ref in enumerate((f0_ref, f1_ref)):
        lb = lb_all[d:d + 1]
        z = f_ref[0]
        f = lb + (1.0 - lb) * jax.nn.sigmoid(z)
        logf = jnp.log(f)
        k = (1.0 - lb) * jax.nn.sigmoid(-z)
        cums = _mm_01x(trio_ref[d], logf)
        bc, tot = cums[0:TM], cums[TM:]
        qd_ref[0, d] = (q * jnp.exp(bc)).astype(BF16)
        kd_ref[0, d] = (k * jnp.exp(tot - bc)).astype(BF16)
        fl_ref[0, d, 0] = jnp.exp(_mm_01x(cind_ref[...], logf))
        bc2 = bc * LOG2E
        for delta in range(HGRN_CHUNK):
            if d == 0:
                sh = delta
                ok = pos >= delta
            else:
                sh = (TM - delta) % TM
                ok = pos + delta <= HGRN_CHUNK - 1
            if delta == 0:
                k_sh, bc_sh, v_sh = k, bc2, v
            else:
                k_sh = pltpu.roll(k, sh, 0)
                bc_sh = pltpu.roll(bc2, sh, 0)
                v_sh = pltpu.roll(v, sh, 0)
            e = q_b * k_sh.astype(BF16) * jnp.exp2(jnp.where(ok, bc2 - bc_sh, NEG)).astype(BF16)
            o_acc = o_acc + _dot(e, hmb_ref[...]) * v_sh
    oin_ref[0] = o_acc


def _hgrn_prep(p, lb_raw, layer):
    b, s, _ = p.shape
    gw = GROUP_WIDTH
    nt = s // TM
    consts = _hgrn_consts()

    def col(c):
        return pl.BlockSpec((1, TM, gw), lambda i, t: (i, t, c // gw))

    dir_spec = pl.BlockSpec((1, 2, TM, gw), lambda i, t: (i, 0, t, 0))
    oin, qd, kd, fl = pl.pallas_call(
        functools.partial(_hgrn_prep_kernel, layer=layer),
        grid=(b, nt),
        in_specs=[col(COL_HG_Q), col(COL_HG_F), col(COL_HG_F + gw), col(COL_HG_I), _full_spec(lb_raw)]
                 + [_full_spec(c) for c in consts],
        out_specs=[pl.BlockSpec((1, TM, gw), lambda i, t: (i, t, 0)), dir_spec, dir_spec,
                   pl.BlockSpec((1, 2, 1, HGRN_PER_TILE, gw), lambda i, t: (i, 0, t, 0, 0))],
        out_shape=[jax.ShapeDtypeStruct((b, s, gw), F32), jax.ShapeDtypeStruct((b, 2, s, gw), BF16),
                   jax.ShapeDtypeStruct((b, 2, s, gw), BF16),
                   jax.ShapeDtypeStruct((b, 2, nt, HGRN_PER_TILE, gw), F32)],
        compiler_params=_cparams(2),
        name="hgrn_prep",
    )(p, p, p, p, lb_raw, *consts)
    return oin, qd, kd, fl


def _values_with_ones(v):
    ones = jnp.ones((v.shape[0], HEAD_DIM), BF16)
    vb = v.astype(BF16)
    parts = []
    for g in range(KV_HEADS):
        parts += [vb[:, g * HEAD_DIM:(g + 1) * HEAD_DIM], ones]
    return jnp.concatenate(parts, axis=-1)


def _group_queries(q, g):
    return jnp.concatenate([q[:, (g * KV_GROUPS + i) * HEAD_DIM:(g * KV_GROUPS + i + 1) * HEAD_DIM]
                            for i in range(KV_GROUPS)], axis=0)


def _gattn_kernel(q_ref, kv_ref, cq_ref, sq_ref, ck_ref, sk_ref, qg_ref, kg_ref, hm4_ref, hm2_ref,
                  o_ref, k_s, v_s, *, n_ctx, t0):
    s = kv_ref.shape[1]
    t = pl.program_id(1)

    @pl.when(t == 0)
    def _():
        def prep(j, carry):
            r0 = pl.multiple_of(j * TM, TM)
            kv = kv_ref[0, pl.ds(r0, TM), :]
            kn = _head_rms(kv[:, 0:KV_WIDTH], hm2_ref[...], kg_ref[...])
            kn = _rope(kn, ck_ref[pl.ds(r0, TM), :], sk_ref[pl.ds(r0, TM), :])
            k_s[pl.ds(r0, TM), :] = kn.astype(BF16)
            v_s[pl.ds(r0, TM), :] = _values_with_ones(kv[:, KV_WIDTH:])
            return carry
        lax.fori_loop(0, s // TM, prep, 0)

    qn = _head_rms(q_ref[0], hm4_ref[...], qg_ref[...])
    qn = (_rope(qn, cq_ref[...], sq_ref[...]) * (HEAD_DIM ** -0.5 * LOG2E)).astype(BF16)

    def attend(nk):
        outs = []
        for h in range(GROUP_HEADS):
            g = h // KV_GROUPS
            sc = _dot_nt(qn[:, h * HEAD_DIM:(h + 1) * HEAD_DIM], k_s[0:nk, g * HEAD_DIM:(g + 1) * HEAD_DIM])
            pr = jnp.exp2(sc - jnp.max(sc, axis=-1, keepdims=True)).astype(BF16)
            ol = _dot(pr, v_s[0:nk, 2 * g * HEAD_DIM:2 * (g + 1) * HEAD_DIM])
            outs.append(ol[:, 0:HEAD_DIM] / ol[:, HEAD_DIM:HEAD_DIM + 1])
        o_ref[0] = jnp.concatenate(outs, axis=-1).astype(o_ref.dtype)

    if t0 == 0:
        @pl.when(t < n_ctx // TM)
        def _():
            attend(n_ctx)

        @pl.when(t >= n_ctx // TM)
        def _():
            attend(s)
    else:
        attend(s)


def _gattn(p, cos256, sin256, q_gain, k_gain, n_ctx, with_ctx):
    b, s, _ = p.shape
    gw = GROUP_WIDTH
    t0 = 0 if with_ctx else n_ctx // TM
    nt = s // TM - t0
    qg = jnp.tile(q_gain, GROUP_HEADS).reshape(1, gw)
    kg = jnp.tile(k_gain, KV_HEADS).reshape(1, KV_WIDTH)
    hm4 = jnp.asarray(_block_diag_ones(gw, HEAD_DIM), BF16)
    hm2 = jnp.asarray(_block_diag_ones(KV_WIDTH, HEAD_DIM), BF16)
    ck, sk = cos256[:, :KV_WIDTH], sin256[:, :KV_WIDTH]
    return pl.pallas_call(
        functools.partial(_gattn_kernel, n_ctx=n_ctx, t0=t0),
        grid=(b, nt),
        in_specs=[pl.BlockSpec((1, TM, gw), lambda i, t: (i, t + t0, COL_ATT_Q // gw)),
                  pl.BlockSpec((1, s, gw), lambda i, t: (i, 0, COL_ATT_KV // gw)),
                  pl.BlockSpec((TM, gw), lambda i, t: (t + t0, 0)),
                  pl.BlockSpec((TM, gw), lambda i, t: (t + t0, 0)),
                  _full_spec(ck), _full_spec(sk), _full_spec(qg), _full_spec(kg),
                  _full_spec(hm4), _full_spec(hm2)],
        out_specs=pl.BlockSpec((1, TM, gw), lambda i, t: (i, t, 0)),
        out_shape=jax.ShapeDtypeStruct((b, nt * TM, gw), BF16),
        scratch_shapes=[pltpu.VMEM((s, KV_WIDTH), BF16), pltpu.VMEM((s, 2 * KV_WIDTH), BF16)],
        compiler_params=_cparams(2),
        name="global_gqa",
    )(p, p, cos256, sin256, ck, sk, qg, kg, hm4, hm2)


def _wattn_kernel(q_ref, kv_ref, cq_ref, sq_ref, ck_ref, sk_ref, sink_ref, o_ref, k_s, v_s, *, n_ctx, t0):
    s = kv_ref.shape[1]
    qb = Q_BLOCK
    band_w = 3 * qb
    t = pl.program_id(1)
    scale = HEAD_DIM ** -0.5

    @pl.when(t == 0)
    def _():
        def prep(j, carry):
            r0 = pl.multiple_of(j * TM, TM)
            kv = kv_ref[0, pl.ds(r0, TM), :]
            kr = _rope(kv[:, 0:KV_WIDTH], ck_ref[pl.ds(r0, TM), :], sk_ref[pl.ds(r0, TM), :])
            k_s[pl.ds(r0, TM), :] = kr.astype(BF16)
            v_s[pl.ds(r0, TM), :] = _values_with_ones(kv[:, KV_WIDTH:])
            return carry
        lax.fori_loop(0, s // TM, prep, 0)

    qr = (_rope(q_ref[0], cq_ref[...], sq_ref[...]) * scale).astype(BF16)
    sink = sink_ref[...]
    grow = lax.broadcasted_iota(jnp.int32, (KV_GROUPS * qb, 1), 0)

    def attend(keys, vals, valid):
        outs = []
        for g in range(KV_HEADS):
            sc = _dot_nt(_group_queries(qr, g), keys[:, g * HEAD_DIM:(g + 1) * HEAD_DIM])
            if valid is not None:
                sc = jnp.where(valid, sc, NEG)
            sk = sink[:, g * KV_GROUPS * HEAD_DIM:g * KV_GROUPS * HEAD_DIM + 1]
            for i in range(1, KV_GROUPS):
                h = g * KV_GROUPS + i
                sk = jnp.where(grow >= i * qb, sink[:, h * HEAD_DIM:h * HEAD_DIM + 1], sk)
            m = jnp.maximum(jnp.max(sc, axis=-1, keepdims=True), sk)
            ol = _dot(jnp.exp(sc - m).astype(BF16), vals[:, 2 * g * HEAD_DIM:2 * (g + 1) * HEAD_DIM])
            o = ol[:, 0:HEAD_DIM] / (ol[:, HEAD_DIM:HEAD_DIM + 1] + jnp.exp(sk - m))
            outs += [o[i * qb:(i + 1) * qb] for i in range(KV_GROUPS)]
        o_ref[0] = jnp.concatenate(outs, axis=-1).astype(o_ref.dtype)

    def ctx_tile():
        attend(k_s[0:n_ctx, :], v_s[0:n_ctx, :], None)

    def latent_tile():
        n = t + t0 - n_ctx // qb
        start = pl.multiple_of(jnp.minimum(n_ctx - qb + qb * n, s - band_w), qb)
        keys = jnp.concatenate([k_s[pl.ds(start, band_w), :], k_s[0:n_ctx, :]], axis=0)
        vals = jnp.concatenate([v_s[pl.ds(start, band_w), :], v_s[0:n_ctx, :]], axis=0)
        shape = (KV_GROUPS * qb, band_w + n_ctx)
        r = lax.broadcasted_iota(jnp.int32, shape, 0) & (qb - 1)
        c = lax.broadcasted_iota(jnp.int32, shape, 1)
        k_pos = start - n_ctx + c
        dist = n * qb + r - k_pos
        valid = (c >= band_w) | ((jnp.abs(dist) <= WINDOW) & (k_pos >= 0))
        attend(keys, vals, valid)

    if t0 == 0:
        pl.when(t < n_ctx // qb)(ctx_tile)
        pl.when(t >= n_ctx // qb)(latent_tile)
    else:
        latent_tile()


def _wattn(p, cos256, sin256, sink, n_ctx, with_ctx):
    b, s, _ = p.shape
    gw = GROUP_WIDTH
    qb = Q_BLOCK
    assert n_ctx >= qb and s - n_ctx >= 3 * qb and n_ctx % qb == 0
    t0 = 0 if with_ctx else n_ctx // qb
    nt = s // qb - t0
    sink_row = jnp.repeat(sink, HEAD_DIM).reshape(1, gw)
    ck, sk = cos256[:, :KV_WIDTH], sin256[:, :KV_WIDTH]
    return pl.pallas_call(
        functools.partial(_wattn_kernel, n_ctx=n_ctx, t0=t0),
        grid=(b, nt),
        in_specs=[pl.BlockSpec((1, qb, gw), lambda i, t: (i, t + t0, COL_SWA_Q // gw)),
                  pl.BlockSpec((1, s, gw), lambda i, t: (i, 0, COL_SWA_KV // gw)),
                  pl.BlockSpec((qb, gw), lambda i, t: (t + t0, 0)),
                  pl.BlockSpec((qb, gw), lambda i, t: (t + t0, 0)),
                  _full_spec(ck), _full_spec(sk), _full_spec(sink_row)],
        out_specs=pl.BlockSpec((1, qb, gw), lambda i, t: (i, t, 0)),
        out_shape=jax.ShapeDtypeStruct((b, nt * qb, gw), BF16),
        scratch_shapes=[pltpu.VMEM((s, KV_WIDTH), BF16), pltpu.VMEM((s, 2 * KV_WIDTH), BF16)],
        compiler_params=_cparams(2),
        name="window_gqa",
    )(p, p, cos256, sin256, ck, sk, sink_row)


def _outproj_kernel(gf_ref, gb_ref, gz_ref, b_ref, c_ref, hf_ref, hb_ref, hg_ref, x_ref, mod_ref,
                    ggain_ref, hgain_ref, gpost_ref, gpre_ref, hmb_ref, w_ref, xo_ref, h_ref):
    gw = GROUP_WIDTH
    hmb = hmb_ref[...]
    mix_a = _head_rms(gf_ref[0] + gb_ref[0], hmb, ggain_ref[...]) * _silu(gz_ref[0])
    mix_d = _head_rms(hf_ref[0] + hb_ref[0], hmb, hgain_ref[...]) * _silu(hg_ref[0])
    o = (_dot(mix_a.astype(BF16), w_ref[0, 0:gw]) + _dot(b_ref[0], w_ref[0, gw:2 * gw])
         + _dot(c_ref[0], w_ref[0, 2 * gw:3 * gw]) + _dot(mix_d.astype(BF16), w_ref[0, 3 * gw:4 * gw]))
    mod = mod_ref[0, 0]
    x = x_ref[0] + mod[2:3] * _rms(o, gpost_ref[...])
    xo_ref[0] = x
    h_ref[0] = (_rms(x, gpre_ref[...]) * (1.0 + mod[4:5]) + mod[3:4]).astype(h_ref.dtype)


def _outproj(gdn_fb, attn_b, attn_c, hgrn_fb, p, xs, modt, gdn_gain, hgrn_gain, g_post, g_pre, w_bf16, t0):
    b, s, d = xs.shape
    gw = GROUP_WIDTH
    nt = s // TM - t0

    def full(col=0):
        return pl.BlockSpec((1, TM, gw), lambda i, t: (i, t + t0, col // gw))

    local = pl.BlockSpec((1, TM, gw), lambda i, t: (i, t, 0))
    row = pl.BlockSpec((1, d), lambda i, t: (0, 0))
    hrow = pl.BlockSpec((1, gw), lambda i, t: (0, 0))
    hmb = jnp.asarray(_block_diag_ones(gw, HEAD_DIM), BF16)
    return pl.pallas_call(
        _outproj_kernel,
        grid=(b, nt),
        in_specs=[full(), full(), full(COL_GDN_Z), local, local, full(), full(), full(COL_HG_G),
                  pl.BlockSpec((1, TM, d), lambda i, t: (i, t + t0, 0)),
                  _mod_spec(modt, d, t0),
                  hrow, hrow, row, row, _full_spec(hmb),
                  pl.BlockSpec((1, 4 * gw, d), lambda i, t: (modt[1], 0, 0))],
        out_specs=[pl.BlockSpec((1, TM, d), lambda i, t: (i, t, 0)),
                   pl.BlockSpec((1, TM, d), lambda i, t: (i, t, 0))],
        out_shape=[jax.ShapeDtypeStruct((b, nt * TM, d), F32),
                   jax.ShapeDtypeStruct((b, nt * TM, d), BF16)],
        compiler_params=_cparams(2),
        name="outproj",
    )(gdn_fb[0], gdn_fb[1], p, attn_b, attn_c, hgrn_fb[0], hgrn_fb[1], p, xs, modt[0],
      jnp.tile(gdn_gain, GROUP_HEADS).reshape(1, gw), jnp.tile(hgrn_gain, GROUP_HEADS).reshape(1, gw),
      g_post.reshape(1, d), g_pre.reshape(1, d), hmb, w_bf16)


def _ffn_kernel(h_ref, hp_ref, hn_ref, x_ref, mod_ref, gain_ref, wup_ref, cw_ref, cb_ref, wdn_ref, o_ref,
                *, ctx_tiles, n_tiles, d_ff, fc):
    t = pl.program_id(1)
    zero_prev = t == 0
    zero_next = t == n_tiles - 1
    if ctx_tiles:
        zero_prev = zero_prev | (t == ctx_tiles)
        zero_next = zero_next | (t == ctx_tiles - 1)
    hh = jnp.concatenate([h_ref[0], hp_ref[0], hn_ref[0]], axis=0)
    row = lax.broadcasted_iota(jnp.int32, (TM, fc), 0)
    acc = jnp.zeros((TM, x_ref.shape[2]), F32)
    for c0 in range(0, d_ff, fc):
        act = None
        for base in (c0, d_ff + c0):
            uu = _dot(hh, wup_ref[0, :, base:base + fc])
            u = uu[0:TM]
            prev_row = jnp.where(zero_prev, 0.0, uu[TM + 15:TM + 16])
            next_row = jnp.where(zero_next, 0.0, uu[TM + 16:TM + 17])
            u_dn = jnp.where(row == 0, prev_row, pltpu.roll(u, 1, 0))
            u_up = jnp.where(row == TM - 1, next_row, pltpu.roll(u, TM - 1, 0))
            y = (u_dn * cw_ref[0:1, base:base + fc] + u * cw_ref[1:2, base:base + fc]
                 + u_up * cw_ref[2:3, base:base + fc] + cb_ref[0:1, base:base + fc])
            act = _silu(y) if act is None else act * y
        acc = acc + _dot(act.astype(BF16), wdn_ref[0, c0:c0 + fc, :])
    mod = mod_ref[0, 0]
    o_ref[0] = x_ref[0] + mod[5:6] * _rms(acc, gain_ref[...])


def _ffn(h2, x_mid, modt, gain, wup_bf16, conv_w, conv_b, wdn_bf16, ctx_tiles):
    b, s, d = x_mid.shape
    d_ff = wdn_bf16.shape[1]
    layer = modt[1]
    fc = d_ff // 2
    assert fc % 128 == 0
    nt = s // TM
    hb = TM // 16
    mod_off = 0 if ctx_tiles else 1
    single = dict(pipeline_mode=pl.Buffered(1))
    return pl.pallas_call(
        functools.partial(_ffn_kernel, ctx_tiles=ctx_tiles, n_tiles=nt, d_ff=d_ff, fc=fc),
        grid=(b, nt),
        in_specs=[pl.BlockSpec((1, TM, d), lambda i, t: (i, t, 0)),
                  pl.BlockSpec((1, 16, d), lambda i, t: (i, jnp.maximum(t * hb - 1, 0), 0)),
                  pl.BlockSpec((1, 16, d), lambda i, t: (i, jnp.minimum((t + 1) * hb, nt * hb - 1), 0)),
                  pl.BlockSpec((1, TM, d), lambda i, t: (i, t, 0)),
                  _mod_spec(modt, d, mod_off),
                  pl.BlockSpec((1, d), lambda i, t: (0, 0)),
                  pl.BlockSpec((1, d, 2 * d_ff), lambda i, t: (layer, 0, 0), **single),
                  pl.BlockSpec((3, 2 * d_ff), lambda i, t: (0, 0)),
                  pl.BlockSpec((1, 2 * d_ff), lambda i, t: (0, 0)),
                  pl.BlockSpec((1, d_ff, d), lambda i, t: (layer, 0, 0), **single)],
        out_specs=pl.BlockSpec((1, TM, d), lambda i, t: (i, t, 0)),
        out_shape=jax.ShapeDtypeStruct((b, s, d), F32),
        compiler_params=_cparams(2),
        name="conv_ffn",
    )(h2, h2, h2, x_mid, modt[0], gain.reshape(1, d), wup_bf16, conv_w, conv_b.reshape(1, -1), wdn_bf16)


def _rope_tables(n_ctx, length):
    rows = length // GRID_W
    row = np.repeat(np.arange(rows), GRID_W).astype(np.float64)
    col = np.tile(np.arange(GRID_W), rows).astype(np.float64)
    n_freq = HEAD_DIM // 4
    inv_freq = 1.0 / (ROPE_THETA ** (np.arange(n_freq, dtype=np.float64) / n_freq))
    ang = np.concatenate([row[:, None] * inv_freq, col[:, None] * inv_freq], axis=-1)
    cos, sin = np.cos(ang), np.sin(ang)
    cos_full = np.repeat(cos, 2, axis=-1)
    sin_signed = np.stack([-sin, sin], axis=-1).reshape(length, HEAD_DIM)
    cos_full = np.concatenate([np.ones((n_ctx, HEAD_DIM)), cos_full], axis=0)
    sin_signed = np.concatenate([np.zeros((n_ctx, HEAD_DIM)), sin_signed], axis=0)
    return (jnp.asarray(np.tile(cos_full, (1, GROUP_HEADS)), F32),
            jnp.asarray(np.tile(sin_signed, (1, GROUP_HEADS)), F32))


def _permute_w_in(w):
    pad = jnp.zeros(w.shape[:-1] + (P_WIDTH - w.shape[-1],), w.dtype)
    return jnp.concatenate([w[..., 0:1024], w[..., 1040:], w[..., 1024:1040], pad], axis=-1)


def kernel(x, c, ctx, c_ctx, ada_w, ada_b, norm_pre_mix, norm_post_mix, norm_pre_ffn, norm_post_ffn, w_in, w_out, gdn_conv_w, gdn_a_log, gdn_dt_bias, gdn_norm, attn_q_norm, attn_k_norm, swa_sink, hgrn_lb_raw, hgrn_norm, ffn_w_up, ffn_conv_w, ffn_conv_b, ffn_w_down):
    bsz, length, d = x.shape
    n_ctx = ctx.shape[1]
    depth = ada_w.shape[0]
    assert n_ctx == TM and length % TM == 0 and w_in.shape[-1] == 3344
    ctx_tiles = n_ctx // TM

    xs = jnp.concatenate([ctx, x], axis=1)
    rows = -(-(bsz + 1) // 8) * 8
    cstack = jnp.concatenate([c, c_ctx[None], jnp.zeros((rows - bsz - 1, d), F32)], axis=0)
    mod_all = _ada_mod(cstack, ada_w, ada_b).reshape(depth, rows, N_MOD, d)
    cos256, sin256 = _rope_tables(n_ctx, length)
    w_in_p = _permute_w_in(w_in).astype(BF16)
    w_out_b = w_out.astype(BF16)
    w_up_b = ffn_w_up.astype(BF16)
    w_dn_b = ffn_w_down.astype(BF16)

    for layer in range(depth):
        with_ctx = layer < depth - 1
        modt = (mod_all, layer, bsz)
        p = _inproj(xs, modt, norm_pre_mix[layer], w_in_p)
        gdn_parts = _gdn_prep(p, gdn_conv_w[layer], gdn_a_log[layer], gdn_dt_bias[layer], n_ctx)
        mix_b = _gattn(p, cos256, sin256, attn_q_norm[layer], attn_k_norm[layer], n_ctx, with_ctx)
        mix_c = _wattn(p, cos256, sin256, swa_sink[layer], n_ctx, with_ctx)
        hgrn_parts = _hgrn_prep(p, hgrn_lb_raw, layer)
        gdn_fb, hgrn_fb = _scans(gdn_parts, hgrn_parts, p, n_ctx)
        t0 = 0 if with_ctx else ctx_tiles
        x_mid, h2 = _outproj(gdn_fb, mix_b, mix_c, hgrn_fb, p, xs, modt, gdn_norm[layer], hgrn_norm[layer],
                             norm_post_mix[layer], norm_pre_ffn[layer], w_out_b, t0)
        xs = _ffn(h2, x_mid, modt, norm_post_ffn[layer], w_up_b, ffn_conv_w[layer],
                  ffn_conv_b[layer], w_dn_b, ctx_tiles if with_ctx else 0)
    return xs
```

```python
import functools

import numpy as np
import jax
import jax.numpy as jnp
from jax import lax
from jax.experimental import pallas as pl
from jax.experimental.pallas import tpu as pltpu

F32 = jnp.float32
BF16 = jnp.bfloat16

HEAD_DIM = 64
GROUP_HEADS = 4
GROUP_WIDTH = GROUP_HEADS * HEAD_DIM
KV_HEADS = 2
KV_WIDTH = KV_HEADS * HEAD_DIM
KV_GROUPS = GROUP_HEADS // KV_HEADS
GRID_W = 64
GDN_CHUNK = 64
HGRN_CHUNK = 16
Q_BLOCK = 128
WINDOW = 128
ROPE_THETA = 10000.0
N_MOD = 6
EPS = 1e-6
NEG = -1e30
TM = 256
GDN_PER_TILE = TM // GDN_CHUNK
HGRN_PER_TILE = TM // HGRN_CHUNK
LOG2E = 1.4426950408889634
ROW_PAD = 16
VMEM_LIMIT = 56 * 1024 * 1024

COL_GDN_QKV = 0
COL_GDN_Z = 768
COL_ATT_Q = 1024
COL_ATT_KV = 1280
COL_SWA_Q = 1536
COL_SWA_KV = 1792
COL_HG_Q = 2048
COL_HG_F = 2304
COL_HG_I = 2816
COL_HG_G = 3072
COL_GDN_BA = 3328
P_WIDTH = 3456


def _cparams(n_axes):
    return pltpu.CompilerParams(dimension_semantics=("arbitrary",) * n_axes,
                                vmem_limit_bytes=VMEM_LIMIT)


def _dot(a, b):
    return jnp.dot(a, b, preferred_element_type=F32)


def _dot_nt(a, b):
    return lax.dot_general(a, b, (((1,), (1,)), ((), ())), preferred_element_type=F32)


def _dot_tn(a, b):
    return lax.dot_general(a, b, (((0,), (0,)), ((), ())), preferred_element_type=F32)


def _split(a, parts):
    out = []
    r = a
    for i in range(parts):
        p = r.astype(BF16)
        out.append(p)
        if i + 1 < parts:
            r = r - p.astype(F32)
    return out


def _mm_x01(a, b01, parts=3):
    m = a.shape[0]
    r = _dot(jnp.concatenate(_split(a, parts), axis=0), b01)
    return sum(r[i * m:(i + 1) * m] for i in range(parts))


def _mm_01x(a01, b, parts=3):
    n = b.shape[1]
    r = _dot(a01, jnp.concatenate(_split(b, parts), axis=1))
    return sum(r[:, i * n:(i + 1) * n] for i in range(parts))


def _mm_hi(a, b):
    ah, al = _split(a, 2)
    bh, bl = _split(b, 2)
    return _dot(ah, bh) + _dot(al, bh) + _dot(ah, bl)


def _silu(x):
    return x * jax.nn.sigmoid(x)


def _softplus(x):
    return jnp.maximum(x, 0.0) + jnp.log1p(jnp.exp(-jnp.abs(x)))


def _head_sumsq(x, ones_bd):
    return _mm_x01(x * x, ones_bd, parts=2)


def _head_rms(x, ones_bd, gain):
    return x * lax.rsqrt(_head_sumsq(x, ones_bd) * (1.0 / HEAD_DIM) + EPS) * gain


def _rope(x, cos, sin_signed):
    w = x.shape[-1]
    lane = lax.broadcasted_iota(jnp.int32, x.shape, x.ndim - 1)
    nxt = pltpu.roll(x, w - 1, x.ndim - 1)
    prv = pltpu.roll(x, 1, x.ndim - 1)
    swapped = jnp.where((lane & 1) == 0, nxt, prv)
    return x * cos + swapped * sin_signed


def _stack_heads(x_bf16, hm_bf16):
    return jnp.concatenate([x_bf16] * GROUP_HEADS, axis=0) * hm_bf16


def _mm_heads(a, b, hm_bf16):
    c = a.shape[0]
    r = _dot(jnp.concatenate(_split(a, 2), axis=0), _stack_heads(b.astype(BF16), hm_bf16))
    return r[0:c] + r[c:2 * c]


def _block_diag_ones(n, blk):
    i = np.arange(n)
    return (i[:, None] // blk == i[None, :] // blk).astype(np.float32)


def _gdn_consts():
    cs = GDN_CHUNK
    hm = _block_diag_ones(GROUP_WIDTH, HEAD_DIM)
    r = np.arange(TM)
    same = r[:, None] // cs == r[None, :] // cs
    tri_rows = np.stack([same & (r[None, :] <= r[:, None]), same & (r[None, :] >= r[:, None])])
    tri_ones = np.concatenate([tri_rows, np.broadcast_to(same, (2, TM, TM))], axis=1).astype(np.float32)
    tri_lanes = np.stack([same & (r[:, None] <= r[None, :]), same & (r[:, None] >= r[None, :])]).astype(np.float32)
    ea = np.zeros((2, 128, GROUP_WIDTH), np.float32)
    eb = np.zeros((2, 128, GROUP_WIDTH), np.float32)
    for d in range(2):
        for h in range(GROUP_HEADS):
            eb[d, d * 4 + h, h * 64:(h + 1) * 64] = 1.0
            ea[d, 8 + d * 4 + h, h * 64:(h + 1) * 64] = 1.0
    i = np.arange(cs)[:, None]
    j = np.arange(GROUP_WIDTH)[None, :] % cs
    incl = np.stack([j <= i, j >= i]).astype(np.float32)
    strict = np.stack([j < i, j > i]).astype(np.float32)
    lvl = np.stack([((i >> (l + 1)) == (j >> (l + 1))) & ((i >> l) != (j >> l)) for l in range(6)]).astype(np.float32)
    eye = (i == j).astype(np.float32)
    last = np.stack([same & (r[:, None] % cs == e) for e in (cs - 1, 0)]).astype(np.float32)
    return [jnp.asarray(hm, BF16), jnp.asarray(tri_ones, BF16), jnp.asarray(tri_lanes, BF16),
            jnp.asarray(ea, BF16), jnp.asarray(eb, BF16), jnp.asarray(incl, F32), jnp.asarray(strict, F32),
            jnp.asarray(lvl, F32), jnp.asarray(eye, F32), jnp.asarray(last, BF16)]


def _hgrn_consts():
    r = np.arange(TM)
    same = r[:, None] // HGRN_CHUNK == r[None, :] // HGRN_CHUNK
    tri = np.stack([same & (r[None, :] <= r[:, None]), same & (r[None, :] >= r[:, None])])
    tri_ones = np.concatenate([tri, np.broadcast_to(same, (2, TM, TM))], axis=1).astype(np.float32)
    chunk_ind = (np.arange(HGRN_PER_TILE)[:, None] == r[None, :] // HGRN_CHUNK).astype(np.float32)
    return [jnp.asarray(_block_diag_ones(TM, HEAD_DIM), BF16), jnp.asarray(tri_ones, BF16),
            jnp.asarray(chunk_ind, BF16)]


def _full_spec(arr):
    nd = arr.ndim
    return pl.BlockSpec(arr.shape, lambda *_: (0,) * nd)


def _mod_spec(modt, d, tile_off):
    _, layer, bsz = modt
    return pl.BlockSpec((1, 1, N_MOD, d), lambda i, t: (layer, jnp.where(t + tile_off < 1, bsz, i), 0, 0))


def _flip_tile(t, ctx_tiles, n_tiles):
    return jnp.where(t < ctx_tiles, ctx_tiles - 1 - t, n_tiles - 1 + ctx_tiles - t)


def _ada_kernel(c_ref, w_ref, b_ref, o_ref):
    act = _silu(c_ref[...])
    o_ref[0] = _mm_hi(act, w_ref[0]) + b_ref[0]


def _ada_mod(cstack, ada_w, ada_b):
    depth, d, n = ada_w.shape
    tn = 1536
    rows = cstack.shape[0]
    return pl.pallas_call(
        _ada_kernel,
        grid=(depth, n // tn),
        in_specs=[pl.BlockSpec((rows, d), lambda l, j: (0, 0)),
                  pl.BlockSpec((1, d, tn), lambda l, j: (l, 0, j)),
                  pl.BlockSpec((1, 1, tn), lambda l, j: (l, 0, j))],
        out_specs=pl.BlockSpec((1, rows, tn), lambda l, j: (l, 0, j)),
        out_shape=jax.ShapeDtypeStruct((depth, rows, n), F32),
        compiler_params=_cparams(2),
        name="ada_mod",
    )(cstack, ada_w, ada_b.reshape(depth, 1, n))


def _rms(x, gain):
    return x * lax.rsqrt(jnp.mean(x * x, axis=-1, keepdims=True) + EPS) * gain


def _inproj_kernel(x_ref, mod_ref, gain_ref, w_ref, o_ref):
    mod = mod_ref[0, 0]
    h = _rms(x_ref[0], gain_ref[...]) * (1.0 + mod[1:2]) + mod[0:1]
    o_ref[0] = _dot(h.astype(BF16), w_ref[0])


def _inproj(xs, modt, gain, w_bf16):
    b, s, d = xs.shape
    n = w_bf16.shape[2]
    layer = modt[1]
    return pl.pallas_call(
        _inproj_kernel,
        grid=(b, s // TM),
        in_specs=[pl.BlockSpec((1, TM, d), lambda i, t: (i, t, 0)),
                  _mod_spec(modt, d, 0),
                  pl.BlockSpec((1, d), lambda i, t: (0, 0)),
                  pl.BlockSpec((1, d, n), lambda i, t: (layer, 0, 0))],
        out_specs=pl.BlockSpec((1, TM, n), lambda i, t: (i, t, 0)),
        out_shape=jax.ShapeDtypeStruct((b, s, n), F32),
        compiler_params=_cparams(2),
        name="inproj",
    )(xs, modt[0], gain.reshape(1, d), w_bf16)


def _gdn_prep_kernel(qkv_ref, prev_ref, next_ref, ba_ref, arow_ref, cw_ref, alog_ref, dt_ref, alog_row_ref,
                     dt_row_ref, hmb_ref, trio_ref, tril_ref, ea_ref, eb_ref, incl_ref, strict_ref, lvl_ref,
                     eye_ref, last_ref, u_ref, w_ref, at_ref, qd_ref, kd_ref, gl_ref, *, ctx_tiles, n_tiles):
    gw = GROUP_WIDTH
    cs = GDN_CHUNK
    t = pl.program_id(1)
    hmb = hmb_ref[...]

    x = qkv_ref[0]
    zero_prev = (t == 0) | (t == ctx_tiles)
    zero_next = (t == ctx_tiles - 1) | (t == n_tiles - 1)
    prev_row = jnp.where(zero_prev, 0.0, prev_ref[0, 7:8])
    next_row = jnp.where(zero_next, 0.0, next_ref[0, 0:1])
    row = lax.broadcasted_iota(jnp.int32, x.shape, 0)
    x_dn = jnp.where(row == 0, prev_row, pltpu.roll(x, 1, 0))
    x_up = jnp.where(row == TM - 1, next_row, pltpu.roll(x, TM - 1, 0))
    y = _silu(x_dn * cw_ref[0:1] + x * cw_ref[1:2] + x_up * cw_ref[2:3])
    q, k, v = y[:, 0:gw], y[:, gw:2 * gw], y[:, 2 * gw:3 * gw]
    q = q * lax.rsqrt(_head_sumsq(q, hmb) + EPS) * (HEAD_DIM ** -0.5)
    k = k * lax.rsqrt(_head_sumsq(k, hmb) + EPS)

    ba = ba_ref[0]
    beta_c = jax.nn.sigmoid(ba)
    g_c = -jnp.exp(alog_ref[...]) * _softplus(ba + dt_ref[...])

    chains = []
    for d in range(2):
        cums = _mm_01x(trio_ref[d], g_c)
        nat = _mm_x01(cums, ea_ref[d])
        gn_gc, gn_tot = nat[0:TM], nat[TM:]
        gn_beta = _mm_x01(beta_c, eb_ref[d])
        g_row = -jnp.exp(alog_row_ref[d:d + 1]) * _softplus(arow_ref[0, d, 0] + dt_row_ref[d:d + 1])
        gc_row = _mm_x01(g_row, tril_ref[d])
        e_gc = jnp.exp(gn_gc)
        qd_ref[0, d] = (q * e_gc).astype(BF16)
        kd_ref[0, d] = (k * jnp.exp(gn_tot - gn_gc)).astype(BF16)
        vb = v * gn_beta
        kbg = k * gn_beta * e_gc
        gl_ref[0, d, 0] = jnp.exp(_mm_x01(gc_row, last_ref[d]))
        for c in range(GDN_PER_TILE):
            sl = slice(c * cs, (c + 1) * cs)
            decay = jnp.exp(jnp.where(incl_ref[d] > 0.5, gn_gc[sl] - gc_row[c:c + 1], NEG))
            kst = _stack_heads(k[sl].astype(BF16), hmb)
            qk_kk = _dot_nt(jnp.concatenate([q[sl], k[sl]], axis=0).astype(BF16), kst)
            at_ref[0, d, sl, :] = (qk_kk[0:cs] * decay).astype(BF16)
            chains.append((d, sl, strict_ref[d] * gn_beta[sl] * qk_kk[cs:] * decay, vb[sl], kbg[sl]))

    tmats = [eye_ref[...] - ch[2] * lvl_ref[0] for ch in chains]
    for lv in range(1, 6):
        tcs = [_mm_heads(tm, ch[2] * lvl_ref[lv], hmb) for tm, ch in zip(tmats, chains)]
        tmats = [tm - _mm_heads(tc, tm, hmb) for tm, tc in zip(tmats, tcs)]
    for tm, (d, sl, _, vb_c, kbg_c) in zip(tmats, chains):
        u_ref[0, d, sl, :] = _mm_heads(tm, vb_c, hmb)
        w_ref[0, d, sl, :] = _mm_heads(tm, kbg_c, hmb).astype(BF16)


def _gdn_scan_kernel(uf, wf, af, qf, kf, gf, ub, wb, ab, qb, kb, gb, hm_ref, hmb_ref, gof_ref, gob_ref, gst_s):
    gcs = GDN_CHUNK

    @pl.when(pl.program_id(1) == 0)
    def _():
        gst_s[...] = jnp.zeros_like(gst_s)

    def gdn_chunk(refs, state, c):
        u_ref, w_ref, a_ref, q_ref, k_ref, g_ref = refs
        sl = slice(c * gcs, (c + 1) * gcs)
        r = _dot(jnp.concatenate([w_ref[0, 0, sl, :], q_ref[0, 0, sl, :]], axis=0), state.astype(BF16))
        v_new = (u_ref[0, 0, sl, :] - r[0:gcs]).astype(BF16)
        o = r[gcs:] + _dot(a_ref[0, 0, sl, :], _stack_heads(v_new, hmb_ref[...]))
        return o, state * g_ref[0, 0, 0, c:c + 1, :] + hm_ref[...] * _dot_tn(k_ref[0, 0, sl, :], v_new)

    g_f, g_b = gst_s[0], gst_s[1]
    for c in range(GDN_PER_TILE):
        cb = GDN_PER_TILE - 1 - c
        o_f, g_f = gdn_chunk((uf, wf, af, qf, kf, gf), g_f, c)
        o_b, g_b = gdn_chunk((ub, wb, ab, qb, kb, gb), g_b, cb)
        gof_ref[0, c * gcs:(c + 1) * gcs, :] = o_f
        gob_ref[0, cb * gcs:(cb + 1) * gcs, :] = o_b
    gst_s[0] = g_f
    gst_s[1] = g_b


def _hgrn_scan_kernel(hqf, hkf, hff, hvf, hoin, hqb, hkb, hfb, hvb, hm_ref, hof_ref, hob_ref, hst_s):
    hcs = HGRN_CHUNK

    @pl.when(pl.program_id(1) == 0)
    def _():
        hst_s[...] = jnp.zeros_like(hst_s)

    def hgrn_chunk(q_ref, k_ref, f_ref, v_ref, st, c):
        sl = slice(c * hcs, (c + 1) * hcs)
        o = _dot_nt(q_ref[0, 0, sl, :], st.astype(BF16))
        upd = _dot_tn(v_ref[0, sl, :].astype(BF16), k_ref[0, 0, sl, :])
        return o, st * f_ref[0, 0, 0, c:c + 1, :] + hm_ref[...] * upd

    h_f, h_b = hst_s[0], hst_s[1]
    for j in range(HGRN_PER_TILE):
        jb = HGRN_PER_TILE - 1 - j
        o_f, h_f = hgrn_chunk(hqf, hkf, hff, hvf, h_f, j)
        o_b, h_b = hgrn_chunk(hqb, hkb, hfb, hvb, h_b, jb)
        hof_ref[0, j * hcs:(j + 1) * hcs, :] = o_f + hoin[0, j * hcs:(j + 1) * hcs, :]
        hob_ref[0, jb * hcs:(jb + 1) * hcs, :] = o_b
    hst_s[0] = h_f
    hst_s[1] = h_b


def _scans(gdn_parts, hgrn_parts, p, n_ctx):
    u, w, at, qd, kd, gl = gdn_parts
    oin, hqd, hkd, hfl = hgrn_parts
    b, s, gw = oin.shape
    nt = s // TM
    ctx_tiles = n_ctx // TM

    def flip(t):
        return _flip_tile(t, ctx_tiles, nt)

    fwd = pl.BlockSpec((1, 1, TM, gw), lambda i, t: (i, 0, t, 0))
    bwd = pl.BlockSpec((1, 1, TM, gw), lambda i, t: (i, 1, flip(t), 0))

    def rows_spec(n_rows, d):
        return pl.BlockSpec((1, 1, 1, n_rows, gw), lambda i, t: (i, d, flip(t) if d else t, 0, 0))

    v_f = pl.BlockSpec((1, TM, gw), lambda i, t: (i, t, COL_HG_I // gw))
    v_b = pl.BlockSpec((1, TM, gw), lambda i, t: (i, flip(t), COL_HG_I // gw))
    out_f = pl.BlockSpec((1, TM, gw), lambda i, t: (i, t, 0))
    out_b = pl.BlockSpec((1, TM, gw), lambda i, t: (i, flip(t), 0))
    hm = jnp.asarray(_block_diag_ones(gw, HEAD_DIM), F32)
    hmb = hm.astype(BF16)
    common = dict(grid=(b, nt), out_specs=[out_f, out_b], out_shape=[jax.ShapeDtypeStruct((b, s, gw), F32)] * 2,
                  scratch_shapes=[pltpu.VMEM((2, gw, gw), F32)], compiler_params=_cparams(2))
    gdn_fb = pl.pallas_call(
        _gdn_scan_kernel,
        in_specs=[fwd] * 5 + [rows_spec(ROW_PAD, 0)] + [bwd] * 5 + [rows_spec(ROW_PAD, 1)]
                 + [_full_spec(hm), _full_spec(hmb)],
        name="gdn_scan", **common,
    )(u, w, at, qd, kd, gl, u, w, at, qd, kd, gl, hm, hmb)
    hgrn_fb = pl.pallas_call(
        _hgrn_scan_kernel,
        in_specs=[fwd, fwd, rows_spec(HGRN_PER_TILE, 0), v_f, out_f, bwd, bwd, rows_spec(HGRN_PER_TILE, 1), v_b,
                  _full_spec(hm)],
        name="hgrn_scan", **common,
    )(hqd, hkd, hfl, p, oin, hqd, hkd, hfl, p, hm)
    return tuple(gdn_fb), tuple(hgrn_fb)


def _gdn_prep(p, conv_w, a_log, dt_bias, n_ctx):
    b, s, _ = p.shape
    gw = GROUP_WIDTH
    nt = s // TM
    ctx_tiles = n_ctx // TM
    consts = _gdn_consts()
    alog_c = jnp.zeros((1, 128), F32).at[0, 8:16].set(a_log.reshape(-1))
    dt_c = jnp.zeros((1, 128), F32).at[0, 8:16].set(dt_bias.reshape(-1))
    alog_row = jnp.repeat(a_log, GDN_CHUNK, axis=1)
    dt_row = jnp.repeat(dt_bias, GDN_CHUNK, axis=1)
    a_raw = p[:, :, COL_GDN_BA + 8:COL_GDN_BA + 16].reshape(b, nt, GDN_PER_TILE, GDN_CHUNK, 2, GROUP_HEADS)
    arow = a_raw.transpose(0, 4, 1, 2, 5, 3).reshape(b, 2, nt, GDN_PER_TILE, gw)
    arow = jnp.pad(arow, ((0, 0), (0, 0), (0, 0), (0, ROW_PAD - GDN_PER_TILE), (0, 0)))
    hb = TM // 8
    dir_spec = pl.BlockSpec((1, 2, TM, gw), lambda i, t: (i, 0, t, 0))
    u, w, at, qd, kd, gl = pl.pallas_call(
        functools.partial(_gdn_prep_kernel, ctx_tiles=ctx_tiles, n_tiles=nt),
        grid=(b, nt),
        in_specs=[pl.BlockSpec((1, TM, 3 * gw), lambda i, t: (i, t, COL_GDN_QKV // (3 * gw))),
                  pl.BlockSpec((1, 8, 3 * gw), lambda i, t: (i, jnp.maximum(t * hb - 1, 0), 0)),
                  pl.BlockSpec((1, 8, 3 * gw), lambda i, t: (i, jnp.minimum((t + 1) * hb, nt * hb - 1), 0)),
                  pl.BlockSpec((1, TM, 128), lambda i, t: (i, t, COL_GDN_BA // 128)),
                  pl.BlockSpec((1, 2, 1, ROW_PAD, gw), lambda i, t: (i, 0, t, 0, 0)),
                  _full_spec(conv_w), _full_spec(alog_c), _full_spec(dt_c), _full_spec(alog_row),
                  _full_spec(dt_row)] + [_full_spec(c) for c in consts],
        out_specs=[dir_spec] * 5 + [pl.BlockSpec((1, 2, 1, ROW_PAD, gw), lambda i, t: (i, 0, t, 0, 0))],
        out_shape=[jax.ShapeDtypeStruct((b, 2, s, gw), F32)] + [jax.ShapeDtypeStruct((b, 2, s, gw), BF16)] * 4
                  + [jax.ShapeDtypeStruct((b, 2, nt, ROW_PAD, gw), F32)],
        compiler_params=_cparams(2),
        name="gdn_prep",
    )(p, p, p, p, arow, conv_w, alog_c, dt_c, alog_row, dt_row, *consts)

    return u, w, at, qd, kd, gl


def _hgrn_prep_kernel(q_ref, f0_ref, f1_ref, i_ref, lbraw_ref, hmb_ref, trio_ref, cind_ref,
                      oin_ref, qd_ref, kd_ref, fl_ref, *, layer):
    depth = lbraw_ref.shape[0]
    raw = [lbraw_ref[l] for l in range(depth)]
    mx = functools.reduce(jnp.maximum, raw)
    ex = [jnp.exp(r - mx) for r in raw]
    lb_all = sum(ex[1:layer + 1], jnp.zeros_like(mx)) / sum(ex)

    q = _silu(q_ref[0])
    v = i_ref[0]
    q_b = q.astype(BF16)
    pos = lax.broadcasted_iota(jnp.int32, (TM, GROUP_WIDTH), 0) % HGRN_CHUNK
    o_acc = jnp.zeros((TM, GROUP_WIDTH), F32)
    for d, f_ref in enumerate((f0_ref, f1_ref)):
        lb = lb_all[d:d + 1]
        z = f_ref[0]
        f = lb + (1.0 - lb) * jax.nn.sigmoid(z)
        logf = jnp.log(f)
        k = (1.0 - lb) * jax.nn.sigmoid(-z)
        cums = _mm_01x(trio_ref[d], logf)
        bc, tot = cums[0:TM], cums[TM:]
        qd_ref[0, d] = (q * jnp.exp(bc)).astype(BF16)
        kd_ref[0, d] = (k * jnp.exp(tot - bc)).astype(BF16)
        fl_ref[0, d, 0] = jnp.exp(_mm_01x(cind_ref[...], logf))
        bc2 = bc * LOG2E
        for delta in range(HGRN_CHUNK):
            if d == 0:
                sh = delta
                ok = pos >= delta
            else:
                sh = (TM - delta) % TM
                ok = pos + delta <= HGRN_CHUNK - 1
            if delta == 0:
                k_sh, bc_sh, v_sh = k, bc2, v
            else:
                k_sh = pltpu.roll(k, sh, 0)
                bc_sh = pltpu.roll(bc2, sh, 0)
                v_sh = pltpu.roll(v, sh, 0)
            e = q_b * k_sh.astype(BF16) * jnp.exp2(jnp.where(ok, bc2 - bc_sh, NEG)).astype(BF16)
            o_acc = o_acc + _dot(e, hmb_ref[...]) * v_sh
    oin_ref[0] = o_acc


def _hgrn_prep(p, lb_raw, layer):
    b, s, _ = p.shape
    gw = GROUP_WIDTH
    nt = s // TM
    consts = _hgrn_consts()

    def col(c):
        return pl.BlockSpec((1, TM, gw), lambda i, t: (i, t, c // gw))

    dir_spec = pl.BlockSpec((1, 2, TM, gw), lambda i, t: (i, 0, t, 0))
    oin, qd, kd, fl = pl.pallas_call(
        functools.partial(_hgrn_prep_kernel, layer=layer),
        grid=(b, nt),
        in_specs=[col(COL_HG_Q), col(COL_HG_F), col(COL_HG_F + gw), col(COL_HG_I), _full_spec(lb_raw)]
                 + [_full_spec(c) for c in consts],
        out_specs=[pl.BlockSpec((1, TM, gw), lambda i, t: (i, t, 0)), dir_spec, dir_spec,
                   pl.BlockSpec((1, 2, 1, HGRN_PER_TILE, gw), lambda i, t: (i, 0, t, 0, 0))],
        out_shape=[jax.ShapeDtypeStruct((b, s, gw), F32), jax.ShapeDtypeStruct((b, 2, s, gw), BF16),
                   jax.ShapeDtypeStruct((b, 2, s, gw), BF16),
                   jax.ShapeDtypeStruct((b, 2, nt, HGRN_PER_TILE, gw), F32)],
        compiler_params=_cparams(2),
        name="hgrn_prep",
    )(p, p, p, p, lb_raw, *consts)
    return oin, qd, kd, fl


def _values_with_ones(v):
    ones = jnp.ones((v.shape[0], HEAD_DIM), BF16)
    vb = v.astype(BF16)
    parts = []
    for g in range(KV_HEADS):
        parts += [vb[:, g * HEAD_DIM:(g + 1) * HEAD_DIM], ones]
    return jnp.concatenate(parts, axis=-1)


def _group_queries(q, g):
    return jnp.concatenate([q[:, (g * KV_GROUPS + i) * HEAD_DIM:(g * KV_GROUPS + i + 1) * HEAD_DIM]
                            for i in range(KV_GROUPS)], axis=0)


def _gattn_kernel(q_ref, kv_ref, cq_ref, sq_ref, ck_ref, sk_ref, qg_ref, kg_ref, hm4_ref, hm2_ref,
                  o_ref, k_s, v_s, *, n_ctx, t0):
    s = kv_ref.shape[1]
    t = pl.program_id(1)

    @pl.when(t == 0)
    def _():
        def prep(j, carry):
            r0 = pl.multiple_of(j * TM, TM)
            kv = kv_ref[0, pl.ds(r0, TM), :]
            kn = _head_rms(kv[:, 0:KV_WIDTH], hm2_ref[...], kg_ref[...])
            kn = _rope(kn, ck_ref[pl.ds(r0, TM), :], sk_ref[pl.ds(r0, TM), :])
            k_s[pl.ds(r0, TM), :] = kn.astype(BF16)
            v_s[pl.ds(r0, TM), :] = _values_with_ones(kv[:, KV_WIDTH:])
            return carry
        lax.fori_loop(0, s // TM, prep, 0)

    qn = _head_rms(q_ref[0], hm4_ref[...], qg_ref[...])
    qn = (_rope(qn, cq_ref[...], sq_ref[...]) * (HEAD_DIM ** -0.5 * LOG2E)).astype(BF16)

    def attend(nk):
        outs = []
        for h in range(GROUP_HEADS):
            g = h // KV_GROUPS
            sc = _dot_nt(qn[:, h * HEAD_DIM:(h + 1) * HEAD_DIM], k_s[0:nk, g * HEAD_DIM:(g + 1) * HEAD_DIM])
            pr = jnp.exp2(sc - jnp.max(sc, axis=-1, keepdims=True)).astype(BF16)
            ol = _dot(pr, v_s[0:nk, 2 * g * HEAD_DIM:2 * (g + 1) * HEAD_DIM])
            outs.append(ol[:, 0:HEAD_DIM] / ol[:, HEAD_DIM:HEAD_DIM + 1])
        o_ref[0] = jnp.concatenate(outs, axis=-1).astype(o_ref.dtype)

    if t0 == 0:
        @pl.when(t < n_ctx // TM)
        def _():
            attend(n_ctx)

        @pl.when(t >= n_ctx // TM)
        def _():
            attend(s)
    else:
        attend(s)


def _gattn(p, cos256, sin256, q_gain, k_gain, n_ctx, with_ctx):
    b, s, _ = p.shape
    gw = GROUP_WIDTH
    t0 = 0 if with_ctx else n_ctx // TM
    nt = s // TM - t0
    qg = jnp.tile(q_gain, GROUP_HEADS).reshape(1, gw)
    kg = jnp.tile(k_gain, KV_HEADS).reshape(1, KV_WIDTH)
    hm4 = jnp.asarray(_block_diag_ones(gw, HEAD_DIM), BF16)
    hm2 = jnp.asarray(_block_diag_ones(KV_WIDTH, HEAD_DIM), BF16)
    ck, sk = cos256[:, :KV_WIDTH], sin256[:, :KV_WIDTH]
    return pl.pallas_call(
        functools.partial(_gattn_kernel, n_ctx=n_ctx, t0=t0),
        grid=(b, nt),
        in_specs=[pl.BlockSpec((1, TM, gw), lambda i, t: (i, t + t0, COL_ATT_Q // gw)),
                  pl.BlockSpec((1, s, gw), lambda i, t: (i, 0, COL_ATT_KV // gw)),
                  pl.BlockSpec((TM, gw), lambda i, t: (t + t0, 0)),
                  pl.BlockSpec((TM, gw), lambda i, t: (t + t0, 0)),
                  _full_spec(ck), _full_spec(sk), _full_spec(qg), _full_spec(kg),
                  _full_spec(hm4), _full_spec(hm2)],
        out_specs=pl.BlockSpec((1, TM, gw), lambda i, t: (i, t, 0)),
        out_shape=jax.ShapeDtypeStruct((b, nt * TM, gw), BF16),
        scratch_shapes=[pltpu.VMEM((s, KV_WIDTH), BF16), pltpu.VMEM((s, 2 * KV_WIDTH), BF16)],
        compiler_params=_cparams(2),
        name="global_gqa",
    )(p, p, cos256, sin256, ck, sk, qg, kg, hm4, hm2)


def _wattn_kernel(q_ref, kv_ref, cq_ref, sq_ref, ck_ref, sk_ref, sink_ref, o_ref, k_s, v_s, *, n_ctx, t0):
    s = kv_ref.shape[1]
    qb = Q_BLOCK
    band_w = 3 * qb
    t = pl.program_id(1)
    scale = HEAD_DIM ** -0.5

    @pl.when(t == 0)
    def _():
        def prep(j, carry):
            r0 = pl.multiple_of(j * TM, TM)
            kv = kv_ref[0, pl.ds(r0, TM), :]
            kr = _rope(kv[:, 0:KV_WIDTH], ck_ref[pl.ds(r0, TM), :], sk_ref[pl.ds(r0, TM), :])
            k_s[pl.ds(r0, TM), :] = kr.astype(BF16)
            v_s[pl.ds(r0, TM), :] = _values_with_ones(kv[:, KV_WIDTH:])
            return carry
        lax.fori_loop(0, s // TM, prep, 0)

    qr = (_rope(q_ref[0], cq_ref[...], sq_ref[...]) * scale).astype(BF16)
    sink = sink_ref[...]
    grow = lax.broadcasted_iota(jnp.int32, (KV_GROUPS * qb, 1), 0)

    def attend(keys, vals, valid):
        outs = []
        for g in range(KV_HEADS):
            sc = _dot_nt(_group_queries(qr, g), keys[:, g * HEAD_DIM:(g + 1) * HEAD_DIM])
            if valid is not None:
                sc = jnp.where(valid, sc, NEG)
            sk = sink[:, g * KV_GROUPS * HEAD_DIM:g * KV_GROUPS * HEAD_DIM + 1]
            for i in range(1, KV_GROUPS):
                h = g * KV_GROUPS + i
                sk = jnp.where(grow >= i * qb, sink[:, h * HEAD_DIM:h * HEAD_DIM + 1], sk)
            m = jnp.maximum(jnp.max(sc, axis=-1, keepdims=True), sk)
            ol = _dot(jnp.exp(sc - m).astype(BF16), vals[:, 2 * g * HEAD_DIM:2 * (g + 1) * HEAD_DIM])
            o = ol[:, 0:HEAD_DIM] / (ol[:, HEAD_DIM:HEAD_DIM + 1] + jnp.exp(sk - m))
            outs += [o[i * qb:(i + 1) * qb] for i in range(KV_GROUPS)]
        o_ref[0] = jnp.concatenate(outs, axis=-1).astype(o_ref.dtype)

    def ctx_tile():
        attend(k_s[0:n_ctx, :], v_s[0:n_ctx, :], None)

    def latent_tile():
        n = t + t0 - n_ctx // qb
        start = pl.multiple_of(jnp.minimum(n_ctx - qb + qb * n, s - band_w), qb)
        keys = jnp.concatenate([k_s[pl.ds(start, band_w), :], k_s[0:n_ctx, :]], axis=0)
        vals = jnp.concatenate([v_s[pl.ds(start, band_w), :], v_s[0:n_ctx, :]], axis=0)
        shape = (KV_GROUPS * qb, band_w + n_ctx)
        r = lax.broadcasted_iota(jnp.int32, shape, 0) & (qb - 1)
        c = lax.broadcasted_iota(jnp.int32, shape, 1)
        k_pos = start - n_ctx + c
        dist = n * qb + r - k_pos
        valid = (c >= band_w) | ((jnp.abs(dist) <= WINDOW) & (k_pos >= 0))
        attend(keys, vals, valid)

    if t0 == 0:
        pl.when(t < n_ctx // qb)(ctx_tile)
        pl.when(t >= n_ctx // qb)(latent_tile)
    else:
        latent_tile()


def _wattn(p, cos256, sin256, sink, n_ctx, with_ctx):
    b, s, _ = p.shape
    gw = GROUP_WIDTH
    qb = Q_BLOCK
    assert n_ctx >= qb and s - n_ctx >= 3 * qb and n_ctx % qb == 0
    t0 = 0 if with_ctx else n_ctx // qb
    nt = s // qb - t0
    sink_row = jnp.repeat(sink, HEAD_DIM).reshape(1, gw)
    ck, sk = cos256[:, :KV_WIDTH], sin256[:, :KV_WIDTH]
    return pl.pallas_call(
        functools.partial(_wattn_kernel, n_ctx=n_ctx, t0=t0),
        grid=(b, nt),
        in_specs=[pl.BlockSpec((1, qb, gw), lambda i, t: (i, t + t0, COL_SWA_Q // gw)),
                  pl.BlockSpec((1, s, gw), lambda i, t: (i, 0, COL_SWA_KV // gw)),
                  pl.BlockSpec((qb, gw), lambda i, t: (t + t0, 0)),
                  pl.BlockSpec((qb, gw), lambda i, t: (t + t0, 0)),
                  _full_spec(ck), _full_spec(sk), _full_spec(sink_row)],
        out_specs=pl.BlockSpec((1, qb, gw), lambda i, t: (i, t, 0)),
        out_shape=jax.ShapeDtypeStruct((b, nt * qb, gw), BF16),
        scratch_shapes=[pltpu.VMEM((s, KV_WIDTH), BF16), pltpu.VMEM((s, 2 * KV_WIDTH), BF16)],
        compiler_params=_cparams(2),
        name="window_gqa",
    )(p, p, cos256, sin256, ck, sk, sink_row)


def _outproj_kernel(gf_ref, gb_ref, gz_ref, b_ref, c_ref, hf_ref, hb_ref, hg_ref, x_ref, mod_ref,
                    ggain_ref, hgain_ref, gpost_ref, gpre_ref, hmb_ref, w_ref, xo_ref, h_ref):
    gw = GROUP_WIDTH
    hmb = hmb_ref[...]
    mix_a = _head_rms(gf_ref[0] + gb_ref[0], hmb, ggain_ref[...]) * _silu(gz_ref[0])
    mix_d = _head_rms(hf_ref[0] + hb_ref[0], hmb, hgain_ref[...]) * _silu(hg_ref[0])
    o = (_dot(mix_a.astype(BF16), w_ref[0, 0:gw]) + _dot(b_ref[0], w_ref[0, gw:2 * gw])
         + _dot(c_ref[0], w_ref[0, 2 * gw:3 * gw]) + _dot(mix_d.astype(BF16), w_ref[0, 3 * gw:4 * gw]))
    mod = mod_ref[0, 0]
    x = x_ref[0] + mod[2:3] * _rms(o, gpost_ref[...])
    xo_ref[0] = x
    h_ref[0] = (_rms(x, gpre_ref[...]) * (1.0 + mod[4:5]) + mod[3:4]).astype(h_ref.dtype)


def _outproj(gdn_fb, attn_b, attn_c, hgrn_fb, p, xs, modt, gdn_gain, hgrn_gain, g_post, g_pre, w_bf16, t0):
    b, s, d = xs.shape
    gw = GROUP_WIDTH
    nt = s // TM - t0

    def full(col=0):
        return pl.BlockSpec((1, TM, gw), lambda i, t: (i, t + t0, col // gw))

    local = pl.BlockSpec((1, TM, gw), lambda i, t: (i, t, 0))
    row = pl.BlockSpec((1, d), lambda i, t: (0, 0))
    hrow = pl.BlockSpec((1, gw), lambda i, t: (0, 0))
    hmb = jnp.asarray(_block_diag_ones(gw, HEAD_DIM), BF16)
    return pl.pallas_call(
        _outproj_kernel,
        grid=(b, nt),
        in_specs=[full(), full(), full(COL_GDN_Z), local, local, full(), full(), full(COL_HG_G),
                  pl.BlockSpec((1, TM, d), lambda i, t: (i, t + t0, 0)),
                  _mod_spec(modt, d, t0),
                  hrow, hrow, row, row, _full_spec(hmb),
                  pl.BlockSpec((1, 4 * gw, d), lambda i, t: (modt[1], 0, 0))],
        out_specs=[pl.BlockSpec((1, TM, d), lambda i, t: (i, t, 0)),
                   pl.BlockSpec((1, TM, d), lambda i, t: (i, t, 0))],
        out_shape=[jax.ShapeDtypeStruct((b, nt * TM, d), F32),
                   jax.ShapeDtypeStruct((b, nt * TM, d), BF16)],
        compiler_params=_cparams(2),
        name="outproj",
    )(gdn_fb[0], gdn_fb[1], p, attn_b, attn_c, hgrn_fb[0], hgrn_fb[1], p, xs, modt[0],
      jnp.tile(gdn_gain, GROUP_HEADS).reshape(1, gw), jnp.tile(hgrn_gain, GROUP_HEADS).reshape(1, gw),
      g_post.reshape(1, d), g_pre.reshape(1, d), hmb, w_bf16)


def _ffn_kernel(h_ref, hp_ref, hn_ref, x_ref, mod_ref, gain_ref, wup_ref, cw_ref, cb_ref, wdn_ref, o_ref,
                *, ctx_tiles, n_tiles, d_ff, fc):
    t = pl.program_id(1)
    zero_prev = t == 0
    zero_next = t == n_tiles - 1
    if ctx_tiles:
        zero_prev = zero_prev | (t == ctx_tiles)
        zero_next = zero_next | (t == ctx_tiles - 1)
    hh = jnp.concatenate([h_ref[0], hp_ref[0], hn_ref[0]], axis=0)
    row = lax.broadcasted_iota(jnp.int32, (TM, fc), 0)
    acc = jnp.zeros((TM, x_ref.shape[2]), F32)
    for c0 in range(0, d_ff, fc):
        act = None
        for base in (c0, d_ff + c0):
            uu = _dot(hh, wup_ref[0, :, base:base + fc])
            u = uu[0:TM]
            prev_row = jnp.where(zero_prev, 0.0, uu[TM + 15:TM + 16])
            next_row = jnp.where(zero_next, 0.0, uu[TM + 16:TM + 17])
            u_dn = jnp.where(row == 0, prev_row, pltpu.roll(u, 1, 0))
            u_up = jnp.where(row == TM - 1, next_row, pltpu.roll(u, TM - 1, 0))
            y = (u_dn * cw_ref[0:1, base:base + fc] + u * cw_ref[1:2, base:base + fc]
                 + u_up * cw_ref[2:3, base:base + fc] + cb_ref[0:1, base:base + fc])
            act = _silu(y) if act is None else act * y
        acc = acc + _dot(act.astype(BF16), wdn_ref[0, c0:c0 + fc, :])
    mod = mod_ref[0, 0]
    o_ref[0] = x_ref[0] + mod[5:6] * _rms(acc, gain_ref[...])


def _ffn(h2, x_mid, modt, gain, wup_bf16, conv_w, conv_b, wdn_bf16, ctx_tiles):
    b, s, d = x_mid.shape
    d_ff = wdn_bf16.shape[1]
    layer = modt[1]
    fc = d_ff
    assert fc % 256 == 0
    nt = s // TM
    hb = TM // 16
    mod_off = 0 if ctx_tiles else 1
    single = dict(pipeline_mode=pl.Buffered(1))
    return pl.pallas_call(
        functools.partial(_ffn_kernel, ctx_tiles=ctx_tiles, n_tiles=nt, d_ff=d_ff, fc=fc),
        grid=(b, nt),
        in_specs=[pl.BlockSpec((1, TM, d), lambda i, t: (i, t, 0)),
                  pl.BlockSpec((1, 16, d), lambda i, t: (i, jnp.maximum(t * hb - 1, 0), 0)),
                  pl.BlockSpec((1, 16, d), lambda i, t: (i, jnp.minimum((t + 1) * hb, nt * hb - 1), 0)),
                  pl.BlockSpec((1, TM, d), lambda i, t: (i, t, 0)),
                  _mod_spec(modt, d, mod_off),
                  pl.BlockSpec((1, d), lambda i, t: (0, 0)),
                  pl.BlockSpec((1, d, 2 * d_ff), lambda i, t: (layer, 0, 0), **single),
                  pl.BlockSpec((3, 2 * d_ff), lambda i, t: (0, 0)),
                  pl.BlockSpec((1, 2 * d_ff), lambda i, t: (0, 0)),
                  pl.BlockSpec((1, d_ff, d), lambda i, t: (layer, 0, 0), **single)],
        out_specs=pl.BlockSpec((1, TM, d), lambda i, t: (i, t, 0)),
        out_shape=jax.ShapeDtypeStruct((b, s, d), F32),
        compiler_params=_cparams(2),
        name="conv_ffn",
    )(h2, h2, h2, x_mid, modt[0], gain.reshape(1, d), wup_bf16, conv_w, conv_b.reshape(1, -1), wdn_bf16)


def _rope_tables(n_ctx, length):
    rows = length // GRID_W
    row = np.repeat(np.arange(rows), GRID_W).astype(np.float64)
    col = np.tile(np.arange(GRID_W), rows).astype(np.float64)
    n_freq = HEAD_DIM // 4
    inv_freq = 1.0 / (ROPE_THETA ** (np.arange(n_freq, dtype=np.float64) / n_freq))
    ang = np.concatenate([row[:, None] * inv_freq, col[:, None] * inv_freq], axis=-1)
    cos, sin = np.cos(ang), np.sin(ang)
    cos_full = np.repeat(cos, 2, axis=-1)
    sin_signed = np.stack([-sin, sin], axis=-1).reshape(length, HEAD_DIM)
    cos_full = np.concatenate([np.ones((n_ctx, HEAD_DIM)), cos_full], axis=0)
    sin_signed = np.concatenate([np.zeros((n_ctx, HEAD_DIM)), sin_signed], axis=0)
    return (jnp.asarray(np.tile(cos_full, (1, GROUP_HEADS)), F32),
            jnp.asarray(np.tile(sin_signed, (1, GROUP_HEADS)), F32))


def _permute_w_in(w):
    pad = jnp.zeros(w.shape[:-1] + (P_WIDTH - w.shape[-1],), w.dtype)
    return jnp.concatenate([w[..., 0:1024], w[..., 1040:], w[..., 1024:1040], pad], axis=-1)


def kernel(x, c, ctx, c_ctx, ada_w, ada_b, norm_pre_mix, norm_post_mix, norm_pre_ffn, norm_post_ffn, w_in, w_out, gdn_conv_w, gdn_a_log, gdn_dt_bias, gdn_norm, attn_q_norm, attn_k_norm, swa_sink, hgrn_lb_raw, hgrn_norm, ffn_w_up, ffn_conv_w, ffn_conv_b, ffn_w_down):
    bsz, length, d = x.shape
    n_ctx = ctx.shape[1]
    depth = ada_w.shape[0]
    assert n_ctx == TM and length % TM == 0 and w_in.shape[-1] == 3344
    ctx_tiles = n_ctx // TM

    xs = jnp.concatenate([ctx, x], axis=1)
    rows = -(-(bsz + 1) // 8) * 8
    cstack = jnp.concatenate([c, c_ctx[None], jnp.zeros((rows - bsz - 1, d), F32)], axis=0)
    mod_all = _ada_mod(cstack, ada_w, ada_b).reshape(depth, rows, N_MOD, d)
    cos256, sin256 = _rope_tables(n_ctx, length)
    w_in_p = _permute_w_in(w_in).astype(BF16)
    w_out_b = w_out.astype(BF16)
    w_up_b = ffn_w_up.astype(BF16)
    w_dn_b = ffn_w_down.astype(BF16)

    for layer in range(depth):
        with_ctx = layer < depth - 1
        modt = (mod_all, layer, bsz)
        p = _inproj(xs, modt, norm_pre_mix[layer], w_in_p)
        gdn_parts = _gdn_prep(p, gdn_conv_w[layer], gdn_a_log[layer], gdn_dt_bias[layer], n_ctx)
        mix_b = _gattn(p, cos256, sin256, attn_q_norm[layer], attn_k_norm[layer], n_ctx, with_ctx)
        mix_c = _wattn(p, cos256, sin256, swa_sink[layer], n_ctx, with_ctx)
        hgrn_parts = _hgrn_prep(p, hgrn_lb_raw, layer)
        gdn_fb, hgrn_fb = _scans(gdn_parts, hgrn_parts, p, n_ctx)
        t0 = 0 if with_ctx else ctx_tiles
        x_mid, h2 = _outproj(gdn_fb, mix_b, mix_c, hgrn_fb, p, xs, modt, gdn_norm[layer], hgrn_norm[layer],
                             norm_post_mix[layer], norm_pre_ffn[layer], w_out_b, t0)
        xs = _ffn(h2, x_mid, modt, norm_post_ffn[layer], w_up_b, ffn_conv_w[layer],
                  ffn_conv_b[layer], w_dn_b, ctx_tiles if with_ctx else 0)
    return xs
```
